```python
import jax, jax.numpy as jnp
from jax import lax
import numpy as np

D_MODEL = 1024
BATCH = 4
SEQ = 4096
DEPTH = 2
DEC_BATCH = 32
DEC_SEQ = 4
PAST_LEN = 8192
PAGE_SIZE = 128

N_MIXERS = 2
N_ATTN_LAYERS = (DEPTH + 1) // 2
N_SSM_LAYERS = DEPTH // 2
EPS = 1e-6
N_HEADS = 8
HEAD_DIM = D_MODEL // N_HEADS
ROT_DIM = HEAD_DIM // 4
ROPE_THETA = 500000.0
MOBA_BLOCK = 256
MOBA_TOPK = 3
MOBA_QBLOCK = 16
SSM_INNER = 2 * D_MODEL
SSM_HEADDIM = 64
SSM_HEADS = SSM_INNER // SSM_HEADDIM
SSM_GROUPS = 4
SSM_HPG = SSM_HEADS // SSM_GROUPS
SSM_STATE = 128
SSM_CONV = 4
SSM_CHUNK = 128
CONV_DIM = SSM_INNER + 2 * SSM_GROUPS * SSM_STATE
IN_PROJ_DIM = SSM_INNER + CONV_DIM + SSM_HEADS
FFN_HIDDEN = -(-8 * D_MODEL // (3 * 256)) * 256

kernel_name = 'moba_mamba2_hybrid_decode_step'


def rms_norm(x, g):
    xf = x.astype(jnp.float32)
    y = xf * lax.rsqrt(jnp.mean(xf * xf, axis=-1, keepdims=True) + EPS)
    return (y * g.astype(jnp.float32)).astype(x.dtype)


def partial_rope(x, pos):
    half = ROT_DIM // 2
    inv = ROPE_THETA ** (-jnp.arange(half, dtype=jnp.float32) * 2.0 / ROT_DIM)
    ang = pos.astype(jnp.float32)[:, None] * inv[None, :]
    cos, sin = jnp.cos(ang), jnp.sin(ang)
    xr = x[..., :ROT_DIM].astype(jnp.float32)
    x1, x2 = xr[..., :half], xr[..., half:]
    rot = jnp.concatenate([x1 * cos - x2 * sin, x2 * cos + x1 * sin], axis=-1).astype(x.dtype)
    return jnp.concatenate([rot, x[..., ROT_DIM:]], axis=-1)


def attn_qkv(h, w_qkv, q_g, k_g, pos):
    b, t, _ = h.shape
    qkv = (h @ w_qkv).reshape(b, t, 3, N_HEADS, HEAD_DIM)
    qkv = jnp.transpose(qkv, (2, 0, 3, 1, 4))
    q = partial_rope(rms_norm(qkv[0], q_g), pos)
    k = partial_rope(rms_norm(qkv[1], k_g), pos)
    return q, k, qkv[2]


def select_blocks(q, kmean, n_full):
    gate = jnp.einsum('bhqd,bhnd->bhqn', q.astype(jnp.float32), kmean)
    gate = jnp.where(jnp.arange(kmean.shape[2]) < n_full, gate, -jnp.inf)
    _, idx = lax.top_k(gate, MOBA_TOPK)
    valid = jnp.arange(MOBA_TOPK) < n_full
    return idx, valid


gather_blocks = jax.vmap(jax.vmap(lambda blocks, i: blocks[i]))


def moba_core(q, sk, sv, valid, lk, lv, lmask):
    scale = HEAD_DIM ** -0.5
    b, h, nq, _ = q.shape
    s_sel = jnp.einsum('bhqd,bhqjrd->bhqjr', q, sk).astype(jnp.float32) * scale
    s_sel = jnp.where(valid[:, None], s_sel, -jnp.inf).reshape(b, h, nq, -1)
    s_loc = jnp.einsum('bhqd,bhmd->bhqm', q, lk).astype(jnp.float32) * scale
    s_loc = jnp.where(lmask, s_loc, -jnp.inf)
    p = jax.nn.softmax(jnp.concatenate([s_sel, s_loc], axis=-1), axis=-1).astype(q.dtype)
    n_sel = s_sel.shape[-1]
    p_sel = p[..., :n_sel].reshape(sk.shape[:-1])
    return (jnp.einsum('bhqjr,bhqjrd->bhqd', p_sel, sv)
            + jnp.einsum('bhqm,bhmd->bhqd', p[..., n_sel:], lv))


def moba_prompt(q, k, v):
    b, h, s, d = q.shape
    nb = -(-s // MOBA_BLOCK)
    pad = nb * MOBA_BLOCK - s
    kp = jnp.pad(k, ((0, 0), (0, 0), (0, pad), (0, 0)))
    vp = jnp.pad(v, ((0, 0), (0, 0), (0, pad), (0, 0)))
    kb = kp.reshape(b, h, nb, MOBA_BLOCK, d)
    vb = vp.reshape(b, h, nb, MOBA_BLOCK, d)
    nbp = max(nb, MOBA_TOPK)
    kmean = jnp.mean(kb.astype(jnp.float32), axis=3)
    kmean = jnp.pad(kmean, ((0, 0), (0, 0), (0, nbp - nb), (0, 0)))

    def one_block(c):
        start = c * MOBA_QBLOCK
        qc = lax.dynamic_slice_in_dim(q, start, MOBA_QBLOCK, axis=2)
        qpos = start + jnp.arange(MOBA_QBLOCK)
        own = start // MOBA_BLOCK
        idx, valid = select_blocks(qc, kmean, own)
        idx = jnp.minimum(idx, nb - 1)
        sk = gather_blocks(kb, idx)
        sv = gather_blocks(vb, idx)
        lk = lax.dynamic_slice_in_dim(kp, own * MOBA_BLOCK, MOBA_BLOCK, axis=2)
        lv = lax.dynamic_slice_in_dim(vp, own * MOBA_BLOCK, MOBA_BLOCK, axis=2)
        kpos = own * MOBA_BLOCK + jnp.arange(MOBA_BLOCK)
        lmask = kpos[None, :] <= qpos[:, None]
        return moba_core(qc, sk, sv, valid, lk, lv, lmask)

    out = lax.map(one_block, jnp.arange(s // MOBA_QBLOCK))
    return jnp.transpose(out, (1, 0, 3, 2, 4)).reshape(b, s, h * d)


def moba_sample(q, k_new, v_new, cache_k_l, cache_v_l, page_table):
    db, h, t, d = q.shape
    n_pages = page_table.shape[1]
    past = n_pages * PAGE_SIZE
    ppb = MOBA_BLOCK // PAGE_SIZE
    n_full = past // MOBA_BLOCK
    nbp = max(n_full, MOBA_TOPK)
    qpos = past + jnp.arange(t)
    rows = cache_k_l[page_table[:, :n_full * ppb]]
    kmean = rows.astype(jnp.float32).reshape(db, n_full, ppb, h, PAGE_SIZE, d).mean(axis=(2, 4))
    kmean = jnp.pad(jnp.transpose(kmean, (0, 2, 1, 3)), ((0, 0), (0, 0), (0, nbp - n_full), (0, 0)))
    idx, valid = select_blocks(q, kmean, n_full)
    lp = jnp.minimum(idx[..., None] * ppb + jnp.arange(ppb), n_pages - 1)
    phys = page_table[jnp.arange(db)[:, None, None, None, None], lp]
    hidx = jnp.arange(h)[None, :, None, None, None]
    sk = cache_k_l[phys, hidx].reshape(db, h, t, MOBA_TOPK, MOBA_BLOCK, d)
    sv = cache_v_l[phys, hidx].reshape(db, h, t, MOBA_TOPK, MOBA_BLOCK, d)
    pt_loc = page_table[:, n_full * ppb:]
    rp = pt_loc.shape[1]
    lk_past = jnp.transpose(cache_k_l[pt_loc], (0, 2, 1, 3, 4)).reshape(db, h, rp * PAGE_SIZE, d)
    lv_past = jnp.transpose(cache_v_l[pt_loc], (0, 2, 1, 3, 4)).reshape(db, h, rp * PAGE_SIZE, d)
    lk = jnp.concatenate([lk_past.astype(k_new.dtype), k_new], axis=2)
    lv = jnp.concatenate([lv_past.astype(v_new.dtype), v_new], axis=2)
    kpos = jnp.concatenate([n_full * MOBA_BLOCK + jnp.arange(rp * PAGE_SIZE), qpos])
    lmask = ((kpos[None, :] <= qpos[:, None])
             & (kpos[None, :] // MOBA_BLOCK == qpos[:, None] // MOBA_BLOCK))
    out = moba_core(q, sk.astype(q.dtype), sv.astype(q.dtype), valid, lk, lv, lmask)
    return jnp.transpose(out, (0, 2, 1, 3)).reshape(db, t, h * d)


def ssd_scan(x, dt, a_head, bm, cm, h0):
    f32 = jnp.float32
    b, t = x.shape[:2]
    L = min(SSM_CHUNK, t)
    pad = (-t) % L
    nc = (t + pad) // L
    padt = lambda z: jnp.pad(z, [(0, 0), (0, pad)] + [(0, 0)] * (z.ndim - 2))
    xdt = padt(x.astype(f32) * dt[..., None])
    a = padt(dt * a_head)
    bf = padt(bm.astype(f32))
    cf = padt(cm.astype(f32))
    xdt = xdt.reshape(b, nc, L, *xdt.shape[2:])
    a = a.reshape(b, nc, L, *a.shape[2:])
    bf = bf.reshape(b, nc, L, *bf.shape[2:])
    cf = cf.reshape(b, nc, L, *cf.shape[2:])
    a_cs = jnp.cumsum(a, axis=2)
    causal = jnp.tril(jnp.ones((L, L), bool))[None, None, :, :, None, None]
    decay = jnp.exp(jnp.where(causal, a_cs[:, :, :, None] - a_cs[:, :, None, :], -jnp.inf))
    cb = jnp.einsum('bclgn,bcsgn->bclsg', cf, bf)
    y_diag = jnp.einsum('bclsg,bclsge,bcsgep->bclgep', cb, decay, xdt)
    decay_end = jnp.exp(a_cs[:, :, -1:] - a_cs)
    chunk_states = jnp.einsum('bclgn,bclge,bclgep->bcgepn', bf, decay_end, xdt)
    chunk_decay = jnp.exp(a_cs[:, :, -1])

    def step(hc, inp):
        s_c, d_c = inp
        return d_c[..., None, None] * hc + s_c, hc

    h_final, h_in = lax.scan(step, h0.astype(f32),
                             (jnp.moveaxis(chunk_states, 1, 0), jnp.moveaxis(chunk_decay, 1, 0)))
    h_in = jnp.moveaxis(h_in, 0, 1)
    y_off = jnp.einsum('bclgn,bcgepn,bclge->bclgep', cf, h_in, jnp.exp(a_cs))
    y = (y_diag + y_off).reshape(b, nc * L, *x.shape[2:])[:, :t]
    return y, h_final


def mamba_mixer(h, conv_prev, ssm_prev, w_in, conv_w, conv_b, dt_bias, a_log, d_skip, gate_norm, w_out):
    b, t, _ = h.shape
    zxbcdt = h @ w_in
    z = zxbcdt[..., :SSM_INNER]
    xbc = zxbcdt[..., SSM_INNER:SSM_INNER + CONV_DIM]
    dt_raw = zxbcdt[..., SSM_INNER + CONV_DIM:]
    full = jnp.concatenate([conv_prev.astype(xbc.dtype), xbc], axis=1)
    new_conv = full[:, t:]
    conv = lax.conv_general_dilated(full, conv_w.astype(full.dtype)[:, None, :], window_strides=(1,),
                                    padding='VALID', dimension_numbers=('NWC', 'WIO', 'NWC'),
                                    feature_group_count=CONV_DIM)
    xbc = jax.nn.silu(conv + conv_b)
    xs = xbc[..., :SSM_INNER].reshape(b, t, SSM_GROUPS, SSM_HPG, SSM_HEADDIM)
    bm = xbc[..., SSM_INNER:SSM_INNER + SSM_GROUPS * SSM_STATE].reshape(b, t, SSM_GROUPS, SSM_STATE)
    cm = xbc[..., SSM_INNER + SSM_GROUPS * SSM_STATE:].reshape(b, t, SSM_GROUPS, SSM_STATE)
    dt = jax.nn.softplus(dt_raw.astype(jnp.float32) + dt_bias.astype(jnp.float32))
    a_head = -jnp.exp(a_log.astype(jnp.float32)).reshape(SSM_GROUPS, SSM_HPG)
    h0 = ssm_prev.reshape(b, SSM_GROUPS, SSM_HPG, SSM_HEADDIM, SSM_STATE)
    y, h_new = ssd_scan(xs, dt.reshape(b, t, SSM_GROUPS, SSM_HPG), a_head, bm, cm, h0)
    y = y + d_skip.astype(jnp.float32).reshape(SSM_GROUPS, SSM_HPG)[..., None] * xs.astype(jnp.float32)
    y = (y.reshape(b, t, SSM_INNER) * jax.nn.silu(z.astype(jnp.float32))).astype(h.dtype)
    gsz = SSM_INNER // SSM_GROUPS
    y = rms_norm(y.reshape(b, t, SSM_GROUPS, gsz), gate_norm.reshape(SSM_GROUPS, gsz)).reshape(b, t, SSM_INNER)
    return y @ w_out, new_conv, h_new.reshape(b, SSM_HEADS, SSM_HEADDIM, SSM_STATE).astype(ssm_prev.dtype)


def swiglu(h, w_gate, w_up, w_down):
    return (jax.nn.silu(h @ w_gate) * (h @ w_up)) @ w_down


def setup_inputs(seed: int = 0) -> dict:
    key = jax.random.key(seed)
    ks = jax.random.split(key, 32)
    f32 = jnp.float32
    n_pages = PAST_LEN // PAGE_SIZE
    n_phys = (DEC_BATCH * n_pages * 5) // 4
    nrm = lambda k, shape, scale: jax.random.normal(k, shape, f32) * scale
    gain = lambda k, shape: 1.0 + 0.05 * jax.random.normal(k, shape, f32)
    x_prompt = nrm(ks[0], (BATCH, SEQ, D_MODEL), 1.0)
    x_sample = nrm(ks[1], (DEC_BATCH, DEC_SEQ, D_MODEL), 1.0)
    cache_k = nrm(ks[2], (N_ATTN_LAYERS, n_phys, N_HEADS, PAGE_SIZE, HEAD_DIM), 1.0)
    cache_v = nrm(ks[3], (N_ATTN_LAYERS, n_phys, N_HEADS, PAGE_SIZE, HEAD_DIM), 1.0)
    page_table = jax.random.permutation(ks[4], n_phys)[:DEC_BATCH * n_pages].reshape(DEC_BATCH, n_pages).astype(jnp.int32)
    state_conv = nrm(ks[5], (N_SSM_LAYERS, DEC_BATCH, SSM_CONV - 1, CONV_DIM), 1.0)
    state_ssm = nrm(ks[6], (N_SSM_LAYERS, DEC_BATCH, SSM_HEADS, SSM_HEADDIM, SSM_STATE), 0.1)
    attn_norm = gain(ks[7], (N_ATTN_LAYERS, D_MODEL))
    w_qkv = nrm(ks[8], (N_ATTN_LAYERS, D_MODEL, 3 * N_HEADS * HEAD_DIM), D_MODEL ** -0.5)
    q_norm = gain(ks[9], (N_ATTN_LAYERS, HEAD_DIM))
    k_norm = gain(ks[10], (N_ATTN_LAYERS, HEAD_DIM))
    w_o = nrm(ks[11], (N_ATTN_LAYERS, N_HEADS * HEAD_DIM, D_MODEL), (N_HEADS * HEAD_DIM) ** -0.5)
    ssm_norm = gain(ks[12], (N_SSM_LAYERS, D_MODEL))
    w_in = nrm(ks[13], (N_SSM_LAYERS, D_MODEL, IN_PROJ_DIM), D_MODEL ** -0.5)
    conv_w = nrm(ks[14], (N_SSM_LAYERS, SSM_CONV, CONV_DIM), SSM_CONV ** -0.5)
    conv_b = nrm(ks[15], (N_SSM_LAYERS, CONV_DIM), 0.02)
    dt0 = jnp.exp(jax.random.uniform(ks[16], (N_SSM_LAYERS, SSM_HEADS), f32, np.log(1e-3), np.log(1e-1)))
    dt_bias = dt0 + jnp.log(-jnp.expm1(-dt0))
    a_log = jnp.log(jax.random.uniform(ks[17], (N_SSM_LAYERS, SSM_HEADS), f32, 1.0, 16.0))
    d_skip = gain(ks[18], (N_SSM_LAYERS, SSM_HEADS))
    gate_norm = gain(ks[19], (N_SSM_LAYERS, SSM_INNER))
    w_out = nrm(ks[20], (N_SSM_LAYERS, SSM_INNER, D_MODEL), SSM_INNER ** -0.5)
    ffn_norm = gain(ks[21], (DEPTH, D_MODEL))
    w_gate = nrm(ks[22], (DEPTH, D_MODEL, FFN_HIDDEN), D_MODEL ** -0.5)
    w_up = nrm(ks[23], (DEPTH, D_MODEL, FFN_HIDDEN), D_MODEL ** -0.5)
    w_down = nrm(ks[24], (DEPTH, FFN_HIDDEN, D_MODEL), FFN_HIDDEN ** -0.5)
    return {'x_prompt': x_prompt, 'x_sample': x_sample, 'cache_k': cache_k, 'cache_v': cache_v,
            'page_table': page_table, 'state_conv': state_conv, 'state_ssm': state_ssm,
            'attn_norm': attn_norm, 'w_qkv': w_qkv, 'q_norm': q_norm, 'k_norm': k_norm, 'w_o': w_o,
            'ssm_norm': ssm_norm, 'w_in': w_in, 'conv_w': conv_w, 'conv_b': conv_b,
            'dt_bias': dt_bias, 'a_log': a_log, 'd_skip': d_skip, 'gate_norm': gate_norm, 'w_out': w_out,
            'ffn_norm': ffn_norm, 'w_gate': w_gate, 'w_up': w_up, 'w_down': w_down}


def reference(x_prompt, x_sample, cache_k, cache_v, page_table, state_conv, state_ssm,
              attn_norm, w_qkv, q_norm, k_norm, w_o,
              ssm_norm, w_in, conv_w, conv_b, dt_bias, a_log, d_skip, gate_norm, w_out,
              ffn_norm, w_gate, w_up, w_down):
    b, s, _ = x_prompt.shape
    db, t, _ = x_sample.shape
    past = page_table.shape[1] * PAGE_SIZE
    pos_p = jnp.arange(s)
    pos_s = past + jnp.arange(t)
    hp, hs = x_prompt, x_sample
    k_p, v_p, k_s, v_s = [], [], [], []
    conv_p, ssm_p, conv_s, ssm_s = [], [], [], []
    for i in range(DEPTH):
        j = i // N_MIXERS
        if i % N_MIXERS == 0:
            q, k, v = attn_qkv(rms_norm(hp, attn_norm[j]), w_qkv[j], q_norm[j], k_norm[j], pos_p)
            hp = hp + moba_prompt(q, k, v) @ w_o[j]
            k_p.append(k)
            v_p.append(v)
            q, k, v = attn_qkv(rms_norm(hs, attn_norm[j]), w_qkv[j], q_norm[j], k_norm[j], pos_s)
            hs = hs + moba_sample(q, k, v, cache_k[j], cache_v[j], page_table) @ w_o[j]
            k_s.append(k)
            v_s.append(v)
        else:
            conv0 = jnp.zeros((b, SSM_CONV - 1, CONV_DIM), state_conv.dtype)
            ssm0 = jnp.zeros((b, SSM_HEADS, SSM_HEADDIM, SSM_STATE), state_ssm.dtype)
            y, c_new, h_new = mamba_mixer(rms_norm(hp, ssm_norm[j]), conv0, ssm0, w_in[j], conv_w[j], conv_b[j],
                                          dt_bias[j], a_log[j], d_skip[j], gate_norm[j], w_out[j])
            hp = hp + y
            conv_p.append(c_new)
            ssm_p.append(h_new)
            y, c_new, h_new = mamba_mixer(rms_norm(hs, ssm_norm[j]), state_conv[j], state_ssm[j], w_in[j], conv_w[j],
                                          conv_b[j], dt_bias[j], a_log[j], d_skip[j], gate_norm[j], w_out[j])
            hs = hs + y
            conv_s.append(c_new)
            ssm_s.append(h_new)
        hp = hp + swiglu(rms_norm(hp, ffn_norm[i]), w_gate[i], w_up[i], w_down[i])
        hs = hs + swiglu(rms_norm(hs, ffn_norm[i]), w_gate[i], w_up[i], w_down[i])
    return (hp, hs, jnp.stack(k_p), jnp.stack(v_p), jnp.stack(k_s), jnp.stack(v_s),
            jnp.stack(conv_p), jnp.stack(ssm_p), jnp.stack(conv_s), jnp.stack(ssm_s))
```

```python
import functools

import jax
import jax.numpy as jnp
from jax import lax
from jax.experimental import pallas as pl
from jax.experimental.pallas import tpu as pltpu

F32 = jnp.float32
BF16 = jnp.bfloat16

EPS = 1e-6
N_HEADS = 8
HEAD_DIM = 128
ROT_DIM = 32
ROPE_THETA = 500000.0
MOBA_BLOCK = 256
MOBA_TOPK = 3
PAGE_SIZE = 128
SSM_INNER = 2048
SSM_HEADDIM = 64
SSM_HEADS = 32
SSM_GROUPS = 4
SSM_HPG = SSM_HEADS // SSM_GROUPS
SSM_STATE = 128
SSM_CONV = 4
CONV_DIM = SSM_INNER + 2 * SSM_GROUPS * SSM_STATE
SSD_CHUNK = 128

LANES = 128
SUBLANES = 8
VMEM_LIMIT_BYTES = 56 * 2**20
PAGES_PER_STEP = 4
SAMPLE_Q_ROWS = 8

_NT = (((1,), (1,)), ((), ()))


def _params(*semantics):
    return pltpu.CompilerParams(dimension_semantics=semantics, vmem_limit_bytes=VMEM_LIMIT_BYTES)


def _rms(x, g):
    return x * lax.rsqrt(jnp.mean(x * x, axis=-1, keepdims=True) + EPS) * g


def _silu(x):
    return x / (1.0 + jnp.exp(-x))


def _qkv_kernel(x_ref, g_ref, w_ref, qg_ref, kg_ref, cos_ref, sa_ref, sb_ref, q_ref, k_ref, v_ref):
    n = _rms(x_ref[...], g_ref[...]).astype(BF16)
    cos, sa, sb = cos_ref[...], sa_ref[...], sb_ref[...]
    d = N_HEADS * HEAD_DIM
    for part, (o_ref, gain_ref) in enumerate(((q_ref, qg_ref), (k_ref, kg_ref), (v_ref, None))):
        y = jnp.dot(n, w_ref[:, part * d:(part + 1) * d], preferred_element_type=F32)
        for h in range(N_HEADS):
            yh = y[:, h * HEAD_DIM:(h + 1) * HEAD_DIM]
            if gain_ref is not None:
                yh = _rms(yh, gain_ref[...])
                yh = (yh * cos + pltpu.roll(yh, HEAD_DIM - ROT_DIM // 2, 1) * sa
                      + pltpu.roll(yh, ROT_DIM // 2, 1) * sb)
            o_ref[h] = yh


def _rope_tables(pos):
    half = ROT_DIM // 2
    inv = ROPE_THETA ** (-jnp.arange(half, dtype=F32) * 2.0 / ROT_DIM)
    ang = pos.astype(F32)[:, None] * inv[None, :]
    cos, sin = jnp.cos(ang), jnp.sin(ang)
    t = pos.shape[0]
    cos_t = jnp.concatenate([cos, cos, jnp.ones((t, HEAD_DIM - ROT_DIM), F32)], axis=1)
    sa = jnp.concatenate([-sin, jnp.zeros((t, HEAD_DIM - half), F32)], axis=1)
    sb = jnp.concatenate([jnp.zeros((t, half), F32), sin, jnp.zeros((t, HEAD_DIM - ROT_DIM), F32)], axis=1)
    return cos_t, sa, sb


def _qkv(x, g, w_bf, qg, kg, pos, tm):
    b, t, d = x.shape
    cos_t, sa, sb = _rope_tables(pos)
    row = lambda v: v.reshape(1, -1)
    out = jax.ShapeDtypeStruct((b, N_HEADS, t, HEAD_DIM), F32)
    tab_spec = pl.BlockSpec((tm, HEAD_DIM), lambda bi, i: (i, 0))
    out_spec = pl.BlockSpec((None, N_HEADS, tm, HEAD_DIM), lambda bi, i: (bi, 0, i, 0))
    const = lambda shape: pl.BlockSpec(shape, lambda bi, i: (0,) * len(shape))
    return pl.pallas_call(
        _qkv_kernel,
        out_shape=(out, out, out),
        grid=(b, t // tm),
        in_specs=[pl.BlockSpec((None, tm, d), lambda bi, i: (bi, i, 0)),
                  const((1, d)), const(w_bf.shape), const((1, HEAD_DIM)), const((1, HEAD_DIM)),
                  tab_spec, tab_spec, tab_spec],
        out_specs=(out_spec, out_spec, out_spec),
        compiler_params=_params("parallel", "parallel"),
        name="qkv_norm_rope",
    )(x, row(g), w_bf, row(qg), row(kg), cos_t, sa, sb)


def _topk_rank(cols, gate, lane):
    rank = jnp.zeros(gate.shape, F32)
    for j, c in enumerate(cols):
        ahead = (c > gate) | ((c == gate) & (j < lane))
        rank = rank + jnp.where(ahead, 1.0, 0.0)
    return rank


def _moba_prompt_kernel(q_ref, k_ref, v_ref, o_ref, kb_ref, vb_ref, km_ref):
    i = pl.program_id(2)
    nb = k_ref.shape[0] // MOBA_BLOCK
    tq = q_ref.shape[0]
    scale = HEAD_DIM ** -0.5

    @pl.when(i == 0)
    def _():
        kb_ref[...] = k_ref[...].astype(BF16)
        vb_ref[...] = v_ref[...].astype(BF16)
        km_ref[...] = jnp.zeros(km_ref.shape, F32)
        for j in range(nb):
            km_ref[j:j + 1, :] = jnp.mean(k_ref[j * MOBA_BLOCK:(j + 1) * MOBA_BLOCK, :], axis=0, keepdims=True)

    qf = q_ref[...]
    qb = qf.astype(BF16)
    gate = lax.dot_general(qf, km_ref[...], _NT, precision=lax.Precision.HIGHEST, preferred_element_type=F32)
    blk = lax.broadcasted_iota(jnp.int32, gate.shape, 1)
    past = blk < i
    gate = jnp.where(past, gate, -jnp.inf)
    rank = _topk_rank([gate[:, j:j + 1] for j in range(nb)], gate, blk)
    bias = jnp.where((rank < MOBA_TOPK) & past, 0.0, -jnp.inf)

    def block(j):
        start = pl.multiple_of(j * MOBA_BLOCK, MOBA_BLOCK)
        return kb_ref[pl.ds(start, MOBA_BLOCK), :], vb_ref[pl.ds(start, MOBA_BLOCK), :]

    kd, vd = block(i)
    s = lax.dot_general(qb, kd, _NT, preferred_element_type=F32) * scale
    qi = lax.broadcasted_iota(jnp.int32, s.shape, 0)
    ki = lax.broadcasted_iota(jnp.int32, s.shape, 1)
    s = jnp.where(ki <= qi, s, -jnp.inf)
    m = jnp.max(s, axis=1, keepdims=True)
    p = jnp.exp(s - m)
    l = jnp.sum(p, axis=1, keepdims=True)
    acc = jnp.dot(p.astype(BF16), vd, preferred_element_type=F32)

    def body(j, carry):
        m, l, acc = carry
        kj, vj = block(j)
        s = lax.dot_general(qb, kj, _NT, preferred_element_type=F32) * scale
        s = s + jnp.sum(jnp.where(blk == j, bias, 0.0), axis=1, keepdims=True)
        m_new = jnp.maximum(m, jnp.max(s, axis=1, keepdims=True))
        alpha = jnp.exp(m - m_new)
        p = jnp.exp(s - m_new)
        l = alpha * l + jnp.sum(p, axis=1, keepdims=True)
        acc = alpha * acc + jnp.dot(p.astype(BF16), vj, preferred_element_type=F32)
        return m_new, l, acc

    m, l, acc = lax.fori_loop(0, i, body, (m, l, acc))
    o_ref[...] = (acc / l).astype(o_ref.dtype)


def _moba_prompt(q, k, v):
    b, h, s, d = q.shape
    assert s % MOBA_BLOCK == 0 and s // MOBA_BLOCK <= LANES
    tq = MOBA_BLOCK
    full = pl.BlockSpec((None, None, s, d), lambda bi, hi, i: (bi, hi, 0, 0))
    return pl.pallas_call(
        _moba_prompt_kernel,
        out_shape=jax.ShapeDtypeStruct((b, s, h * d), BF16),
        grid=(b, h, s // tq),
        in_specs=[pl.BlockSpec((None, None, tq, d), lambda bi, hi, i: (bi, hi, i, 0)), full, full],
        out_specs=pl.BlockSpec((None, tq, d), lambda bi, hi, i: (bi, i, hi)),
        scratch_shapes=[pltpu.VMEM((s, d), BF16), pltpu.VMEM((s, d), BF16),
                        pltpu.VMEM((LANES, d), F32)],
        compiler_params=_params("parallel", "parallel", "arbitrary"),
        name="moba_prompt",
    )(q, k, v)


def _sample_scores_kernel(pt_ref, q_ref, *refs):
    del pt_ref
    k_refs, s_ref = refs[:-1], refs[-1]
    for r, k_ref in enumerate(k_refs):
        for h in range(N_HEADS):
            s = lax.dot_general(q_ref[h].astype(BF16), k_ref[h].astype(BF16), _NT, preferred_element_type=F32)
            s_ref[h, :, r * PAGE_SIZE:(r + 1) * PAGE_SIZE] = s


def _sample_select_kernel(q_ref, kn_ref, vn_ref, s_ref, p_ref, ol_ref, *, n_new, n_full):
    scale = HEAD_DIM ** -0.5
    n_keys = s_ref.shape[-1]
    rows = s_ref.shape[1]
    lane = lax.broadcasted_iota(jnp.int32, (rows, LANES), 1)
    key_blk = (lax.broadcasted_iota(jnp.int32, (rows, n_keys), 1) // MOBA_BLOCK).astype(F32)
    lane_f = lane.astype(F32)
    row = lax.broadcasted_iota(jnp.int32, (rows, 1), 0)
    for h in range(N_HEADS):
        s = s_ref[h]
        cols = [jnp.sum(s[:, j * MOBA_BLOCK:(j + 1) * MOBA_BLOCK], axis=1, keepdims=True) * (1.0 / MOBA_BLOCK)
                for j in range(n_full)]
        gate = jnp.full((rows, LANES), -jnp.inf, F32)
        for j, c in enumerate(cols):
            gate = jnp.where(lane == j, c, gate)
        rank = _topk_rank(cols, gate, lane)
        mask = jnp.zeros(key_blk.shape, jnp.bool_)
        for r in range(min(MOBA_TOPK, n_full)):
            pick = jnp.sum(jnp.where((rank == r) & (lane < n_full), lane_f, 0.0), axis=1, keepdims=True)
            mask = mask | (key_blk == pick)
        s = jnp.where(mask, s * scale, -jnp.inf)
        q = q_ref[h]
        s_new = []
        for u in range(n_new):
            su = jnp.sum(q * kn_ref[h, u:u + 1, :], axis=1, keepdims=True) * scale
            s_new.append(jnp.where(row >= u, su, -jnp.inf))
        m = jnp.max(s, axis=1, keepdims=True)
        for su in s_new:
            m = jnp.maximum(m, su)
        p = jnp.exp(s - m)
        p_new = [jnp.exp(su - m) for su in s_new]
        denom = jnp.sum(p, axis=1, keepdims=True)
        for pu in p_new:
            denom = denom + pu
        p_ref[h] = (p / denom).astype(p_ref.dtype)
        o = jnp.zeros((rows, HEAD_DIM), F32)
        for u, pu in enumerate(p_new):
            o = o + (pu / denom) * vn_ref[h, u:u + 1, :]
        ol_ref[h] = o


def _sample_pv_kernel(pt_ref, p_ref, ol_ref, *refs):
    del pt_ref
    v_refs, o_ref = refs[:-1], refs[-1]

    @pl.when(pl.program_id(1) == 0)
    def _():
        o_ref[...] = ol_ref[...]

    for r, v_ref in enumerate(v_refs):
        for h in range(N_HEADS):
            p = p_ref[h, :, r * PAGE_SIZE:(r + 1) * PAGE_SIZE].astype(BF16)
            o_ref[h] += jnp.dot(p, v_ref[h].astype(BF16), preferred_element_type=F32)


def _page_specs(n):
    def spec(r):
        return pl.BlockSpec((None, N_HEADS, PAGE_SIZE, HEAD_DIM),
                            lambda bi, g, pt: (pt[bi, g * n + r], 0, 0, 0))
    return [spec(r) for r in range(n)]


def _moba_sample(q, k_new, v_new, cache_k, cache_v, page_table):
    db, h, t, d = q.shape
    n_pages = page_table.shape[1]
    assert (n_pages * PAGE_SIZE) % MOBA_BLOCK == 0, "cached keys must end on a MoBA block boundary"
    assert n_pages % PAGES_PER_STEP == 0 and t <= SAMPLE_Q_ROWS
    n_full = n_pages * PAGE_SIZE // MOBA_BLOCK
    n_keys = n_pages * PAGE_SIZE
    rows = SAMPLE_Q_ROWS
    steps = n_pages // PAGES_PER_STEP
    pad = lambda a: jnp.pad(a, ((0, 0), (0, 0), (0, rows - t), (0, 0)))
    qp, knp, vnp = pad(q), pad(k_new), pad(v_new)
    per_seq = pl.BlockSpec((None, h, rows, d), lambda bi, g, pt: (bi, 0, 0, 0))
    keys_spec = pl.BlockSpec((None, h, rows, PAGES_PER_STEP * PAGE_SIZE), lambda bi, g, pt: (bi, 0, 0, g))

    scores = pl.pallas_call(
        _sample_scores_kernel,
        out_shape=jax.ShapeDtypeStruct((db, h, rows, n_keys), F32),
        grid_spec=pltpu.PrefetchScalarGridSpec(
            num_scalar_prefetch=1, grid=(db, steps),
            in_specs=[per_seq] + _page_specs(PAGES_PER_STEP), out_specs=keys_spec),
        compiler_params=_params("parallel", "arbitrary"),
        name="moba_sample_scores",
    )(page_table, qp, *([cache_k] * PAGES_PER_STEP))

    seq4 = lambda last: pl.BlockSpec((None, h, rows, last), lambda bi: (bi, 0, 0, 0))
    probs, o_new = pl.pallas_call(
        functools.partial(_sample_select_kernel, n_new=t, n_full=n_full),
        out_shape=(jax.ShapeDtypeStruct((db, h, rows, n_keys), F32),
                   jax.ShapeDtypeStruct((db, h, rows, d), F32)),
        grid=(db,),
        in_specs=[seq4(d), seq4(d), seq4(d), seq4(n_keys)],
        out_specs=(seq4(n_keys), seq4(d)),
        compiler_params=_params("parallel"),
        name="moba_sample_select",
    )(qp, knp, vnp, scores)

    out = pl.pallas_call(
        _sample_pv_kernel,
        out_shape=jax.ShapeDtypeStruct((db, h, rows, d), F32),
        grid_spec=pltpu.PrefetchScalarGridSpec(
            num_scalar_prefetch=1, grid=(db, steps),
            in_specs=[keys_spec, per_seq] + _page_specs(PAGES_PER_STEP), out_specs=per_seq),
        compiler_params=_params("parallel", "arbitrary"),
        name="moba_sample_pv",
    )(page_table, probs, o_new, *([cache_v] * PAGES_PER_STEP))
    return jnp.transpose(out[:, :, :t], (0, 2, 1, 3)).reshape(db, t, h * d).astype(BF16)


def _proj_ffn_kernel(a_ref, x_ref, wp_ref, g_ref, wg_ref, wu_ref, wd_ref, o_ref, *, chunks):
    hid = x_ref[...] + jnp.dot(a_ref[...], wp_ref[...], preferred_element_type=F32)
    n = _rms(hid, g_ref[...]).astype(BF16)
    width = wg_ref.shape[1] // chunks
    acc = hid
    for c in range(chunks):
        cols = slice(c * width, (c + 1) * width)
        gate = jnp.dot(n, wg_ref[:, cols], preferred_element_type=F32)
        up = jnp.dot(n, wu_ref[:, cols], preferred_element_type=F32)
        act = (_silu(gate) * up).astype(BF16)
        acc = acc + jnp.dot(act, wd_ref[cols, :], preferred_element_type=F32)
    o_ref[...] = acc


def _proj_ffn(a_bf, x, wp_bf, g, wg_bf, wu_bf, wd_bf, tm, chunks=2):
    m, d = x.shape
    ka = a_bf.shape[1]
    const = lambda shape: pl.BlockSpec(shape, lambda i: (0, 0), pipeline_mode=pl.Buffered(1))
    return pl.pallas_call(
        functools.partial(_proj_ffn_kernel, chunks=chunks),
        out_shape=jax.ShapeDtypeStruct((m, d), F32),
        grid=(m // tm,),
        in_specs=[pl.BlockSpec((tm, ka), lambda i: (i, 0)), pl.BlockSpec((tm, d), lambda i: (i, 0)),
                  const(wp_bf.shape), const((1, d)), const(wg_bf.shape), const(wu_bf.shape), const(wd_bf.shape)],
        out_specs=pl.BlockSpec((tm, d), lambda i: (i, 0)),
        compiler_params=_params("parallel"),
        name="proj_ffn",
    )(a_bf, x, wp_bf, g.reshape(1, d), wg_bf, wu_bf, wd_bf)


def _inproj_kernel(x_ref, g_ref, wz_ref, wx_ref, wdt_ref, z_ref, xbc_ref, dt_ref):
    n = _rms(x_ref[...], g_ref[...]).astype(BF16)
    z_ref[...] = jnp.dot(n, wz_ref[...], preferred_element_type=F32)
    xbc_ref[...] = jnp.dot(n, wx_ref[...], preferred_element_type=F32)
    dt_ref[...] = jnp.dot(n, wdt_ref[...], preferred_element_type=F32)


def _inproj(x, g, wz_bf, wx_bf, wdt_bf, tm):
    m, d = x.shape
    const = lambda shape: pl.BlockSpec(shape, lambda i: (0, 0), pipeline_mode=pl.Buffered(1))
    tile = lambda n: pl.BlockSpec((tm, n), lambda i: (i, 0))
    return pl.pallas_call(
        _inproj_kernel,
        out_shape=(jax.ShapeDtypeStruct((m, SSM_INNER), F32), jax.ShapeDtypeStruct((m, CONV_DIM), F32),
                   jax.ShapeDtypeStruct((m, LANES), F32)),
        grid=(m // tm,),
        in_specs=[tile(d), const((1, d)), const(wz_bf.shape), const(wx_bf.shape), const(wdt_bf.shape)],
        out_specs=(tile(SSM_INNER), tile(CONV_DIM), tile(LANES)),
        compiler_params=_params("parallel"),
        name="ssm_in_proj",
    )(x, g.reshape(1, d), wz_bf, wx_bf, wdt_bf)


def _ssd_kernel(*refs, rows, has_init):
    if has_init:
        (z_ref, xbc_ref, dt_ref, cw_ref, cb_ref, dtb_ref, alog_ref, dsk_ref, gn_ref, conv0_ref, ssm0_ref,
         y_ref, hout_ref, buf_ref, act_ref, ysc_ref, xw_ref, ht_ref) = refs
    else:
        (z_ref, xbc_ref, dt_ref, cw_ref, cb_ref, dtb_ref, alog_ref, dsk_ref, gn_ref,
         y_ref, hout_ref, buf_ref, act_ref, ysc_ref, xw_ref, ht_ref) = refs
    L = SSD_CHUNK
    P, N, G, HPG = SSM_HEADDIM, SSM_STATE, SSM_GROUPS, SSM_HPG
    GW = HPG * P
    c = pl.program_id(1)
    last = pl.num_programs(1) - 1
    tail = SUBLANES - (SSM_CONV - 1)

    @pl.when(c == 0)
    def _():
        buf_ref[...] = jnp.zeros(buf_ref.shape, F32)
        if has_init:
            buf_ref[0:SUBLANES, :] = conv0_ref[...]
            for pair in range(SSM_HEADS // 2):
                two = jnp.concatenate([ssm0_ref[2 * pair], ssm0_ref[2 * pair + 1]], axis=0)
                g, off = divmod(2 * pair * P, GW)
                ht_ref[g, :, off:off + 2 * P] = two.T
        else:
            ht_ref[...] = jnp.zeros(ht_ref.shape, F32)

    buf_ref[SUBLANES:SUBLANES + rows, :] = xbc_ref[...]
    conv = cb_ref[...] + buf_ref[tail:tail + L, :] * cw_ref[0:1, :]
    for w in range(1, SSM_CONV):
        conv = conv + buf_ref[tail + w:tail + w + L, :] * cw_ref[w:w + 1, :]
    act_ref[...] = _silu(conv)
    if rows == L:
        buf_ref[0:SUBLANES, :] = buf_ref[L:L + SUBLANES, :]

    x = dt_ref[...] + dtb_ref[...]
    dt = jnp.maximum(x, 0.0) + jnp.log1p(jnp.exp(-jnp.abs(x)))
    if rows < L:
        dt = jnp.where(lax.broadcasted_iota(jnp.int32, dt.shape, 0) < rows, dt, 0.0)
    a = dt * (-jnp.exp(alog_ref[...]))
    ti = lax.broadcasted_iota(jnp.int32, (L, L), 0)
    si = lax.broadcasted_iota(jnp.int32, (L, L), 1)
    causal = ti >= si
    a_cs = jnp.dot(jnp.where(causal, 1.0, 0.0), a, precision=lax.Precision.HIGHEST,
                   preferred_element_type=F32)
    a_cs_t = a_cs.T
    a_tot = a_cs[L - 1:L, :]

    for g in range(G):
        bg = act_ref[:, SSM_INNER + g * N:SSM_INNER + (g + 1) * N]
        cg = act_ref[:, SSM_INNER + G * N + g * N:SSM_INNER + G * N + (g + 1) * N].astype(BF16)
        cb = lax.dot_general(cg, bg.astype(BF16), _NT, preferred_element_type=F32)
        bg_t = bg.T.astype(BF16)
        ht = ht_ref[g]
        y_off = jnp.dot(cg, ht.astype(BF16), preferred_element_type=F32)
        for e8 in range(HPG):
            e = g * HPG + e8
            lanes = slice(e8 * P, (e8 + 1) * P)
            acs = a_cs[:, e:e + 1]
            decay = jnp.exp(jnp.where(causal, acs - a_cs_t[e:e + 1, :], -jnp.inf))
            w = (cb * decay).astype(BF16)
            xs = act_ref[:, e * P:(e + 1) * P]
            xdt = xs * dt[:, e:e + 1]
            y = (jnp.dot(w, xdt.astype(BF16), preferred_element_type=F32)
                 + y_off[:, lanes] * jnp.exp(acs) + dsk_ref[:, e:e + 1] * xs)
            ysc_ref[:, e * P:(e + 1) * P] = y
            xw_ref[:, lanes] = (xdt * jnp.exp(a_tot[:, e:e + 1] - acs)).astype(BF16)
        upd = jnp.dot(bg_t, xw_ref[...], preferred_element_type=F32)
        for e8 in range(HPG):
            e = g * HPG + e8
            lanes = slice(e8 * P, (e8 + 1) * P)
            ht_ref[g, :, lanes] = jnp.exp(a_tot[:, e:e + 1]) * ht[:, lanes] + upd[:, lanes]

    z_in = z_ref[...]
    for g in range(G):
        lanes = slice(g * GW, (g + 1) * GW)
        yg = ysc_ref[0:rows, lanes] * _silu(z_in[:, lanes])
        y_ref[:, lanes] = _rms(yg, gn_ref[:, lanes]).astype(y_ref.dtype)

    @pl.when(c == last)
    def _():
        for pair in range(SSM_HEADS // 2):
            g, off = divmod(2 * pair * P, GW)
            two = ht_ref[g, :, off:off + 2 * P].T
            hout_ref[2 * pair] = two[0:P]
            hout_ref[2 * pair + 1] = two[P:2 * P]


def _ssd(z, xbc, dt, conv_w, conv_b, dt_bias, a_log, d_skip, gate_norm, conv0=None, ssm0=None):
    b, t, _ = z.shape
    has_init = conv0 is not None
    rows = min(SSD_CHUNK, t)
    assert t % rows == 0 and (rows == SSD_CHUNK or t == rows)
    nc = t // rows
    if rows < SSD_CHUNK:
        dt = jnp.pad(dt, ((0, 0), (0, SSD_CHUNK - rows), (0, 0)))
    lane_pad = lambda v: jnp.pad(v.astype(F32), (0, LANES - v.shape[0])).reshape(1, LANES)
    tile = lambda n: pl.BlockSpec((None, rows, n), lambda bi, ci: (bi, ci, 0))
    const = lambda shape: pl.BlockSpec(shape, lambda bi, ci: (0,) * len(shape))
    state_spec = pl.BlockSpec((None, SSM_HEADS, SSM_HEADDIM, SSM_STATE), lambda bi, ci: (bi, 0, 0, 0))
    in_specs = [tile(SSM_INNER), tile(CONV_DIM), pl.BlockSpec((None, SSD_CHUNK, LANES), lambda bi, ci: (bi, ci, 0)),
                const((SSM_CONV, CONV_DIM)), const((1, CONV_DIM)), const((1, LANES)), const((1, LANES)),
                const((1, LANES)), const((1, SSM_INNER))]
    args = [z, xbc, dt, conv_w, conv_b.reshape(1, CONV_DIM), lane_pad(dt_bias), lane_pad(a_log),
            lane_pad(d_skip), gate_norm.reshape(1, SSM_INNER)]
    if has_init:
        in_specs += [pl.BlockSpec((None, SUBLANES, CONV_DIM), lambda bi, ci: (bi, 0, 0)), state_spec]
        args += [jnp.pad(conv0, ((0, 0), (SUBLANES - (SSM_CONV - 1), 0), (0, 0))), ssm0]
    y_dtype = BF16 if rows % 16 == 0 else F32
    return pl.pallas_call(
        functools.partial(_ssd_kernel, rows=rows, has_init=has_init),
        out_shape=(jax.ShapeDtypeStruct((b, t, SSM_INNER), y_dtype),
                   jax.ShapeDtypeStruct((b, SSM_HEADS, SSM_HEADDIM, SSM_STATE), F32)),
        grid=(b, nc),
        in_specs=in_specs,
        out_specs=(tile(SSM_INNER), state_spec),
        scratch_shapes=[pltpu.VMEM((SUBLANES + SSD_CHUNK, CONV_DIM), F32),
                        pltpu.VMEM((SSD_CHUNK, CONV_DIM), F32),
                        pltpu.VMEM((SSD_CHUNK, SSM_INNER), F32),
                        pltpu.VMEM((SSD_CHUNK, SSM_HPG * SSM_HEADDIM), BF16),
                        pltpu.VMEM((SSM_GROUPS, SSM_STATE, SSM_HPG * SSM_HEADDIM), F32)],
        compiler_params=_params("parallel", "arbitrary"),
        name="ssd_chunk",
    )(*args)


def kernel(x_prompt, x_sample, cache_k, cache_v, page_table, state_conv, state_ssm, attn_norm, w_qkv, q_norm, k_norm, w_o, ssm_norm, w_in, conv_w, conv_b, dt_bias, a_log, d_skip, gate_norm, w_out, ffn_norm, w_gate, w_up, w_down):
    b, s, d = x_prompt.shape
    db, t, _ = x_sample.shape
    past = page_table.shape[1] * PAGE_SIZE
    tm_p, tm_s = 512, db * t
    bf = lambda w: w.astype(BF16)

    w_qkv_bf, w_o_bf = bf(w_qkv[0]), bf(w_o[0])
    ffn0 = (ffn_norm[0], bf(w_gate[0]), bf(w_up[0]), bf(w_down[0]))
    q_p, k_p, v_p = _qkv(x_prompt, attn_norm[0], w_qkv_bf, q_norm[0], k_norm[0], jnp.arange(s), tm_p)
    attn_p = _moba_prompt(q_p, k_p, v_p)
    pos_s = past + jnp.tile(jnp.arange(t), db)
    xs_rows = x_sample.reshape(1, db * t, d)
    qkv_s = _qkv(xs_rows, attn_norm[0], w_qkv_bf, q_norm[0], k_norm[0], pos_s, tm_s)
    q_s, k_s, v_s = (jnp.transpose(a.reshape(N_HEADS, db, t, HEAD_DIM), (1, 0, 2, 3)) for a in qkv_s)
    attn_s = _moba_sample(q_s, k_s, v_s, cache_k[0], cache_v[0], page_table)
    h_p = _proj_ffn(attn_p.reshape(b * s, d), x_prompt.reshape(b * s, d), w_o_bf, *ffn0, tm_p)
    h_s = _proj_ffn(attn_s.reshape(db * t, d), x_sample.reshape(db * t, d), w_o_bf, *ffn0, tm_s)

    w_in1 = w_in[0]
    wz_bf = bf(w_in1[:, :SSM_INNER])
    wx_bf = bf(w_in1[:, SSM_INNER:SSM_INNER + CONV_DIM])
    wdt_bf = bf(jnp.pad(w_in1[:, SSM_INNER + CONV_DIM:], ((0, 0), (0, LANES - SSM_HEADS))))
    w_out_bf = bf(w_out[0])
    ffn1 = (ffn_norm[1], bf(w_gate[1]), bf(w_up[1]), bf(w_down[1]))
    ssd_w = (conv_w[0], conv_b[0], dt_bias[0], a_log[0], d_skip[0], gate_norm[0])
    z_p, xbc_p, dt_p = _inproj(h_p, ssm_norm[0], wz_bf, wx_bf, wdt_bf, 256)
    z_s, xbc_s, dt_s = _inproj(h_s, ssm_norm[0], wz_bf, wx_bf, wdt_bf, tm_s)
    xbc_p = xbc_p.reshape(b, s, CONV_DIM)
    xbc_s = xbc_s.reshape(db, t, CONV_DIM)
    y_p, ssm_p = _ssd(z_p.reshape(b, s, SSM_INNER), xbc_p, dt_p.reshape(b, s, LANES), *ssd_w)
    y_s, ssm_s = _ssd(z_s.reshape(db, t, SSM_INNER), xbc_s, dt_s.reshape(db, t, LANES), *ssd_w,
                      conv0=state_conv[0], ssm0=state_ssm[0])
    out_p = _proj_ffn(y_p.reshape(b * s, SSM_INNER), h_p, w_out_bf, *ffn1, tm_p)
    out_s = _proj_ffn(bf(y_s).reshape(db * t, SSM_INNER), h_s, w_out_bf, *ffn1, tm_s)

    keep = SSM_CONV - 1
    conv_p = xbc_p[:, s - keep:]
    conv_s = jnp.concatenate([state_conv[0], xbc_s], axis=1)[:, t:]
    return (out_p.reshape(b, s, d), out_s.reshape(db, t, d), k_p[None], v_p[None], k_s[None], v_s[None],
            conv_p[None], ssm_p[None], conv_s[None], ssm_s[None])
```

```python
import functools

import jax
import jax.numpy as jnp
from jax import lax
from jax.experimental import pallas as pl
from jax.experimental.pallas import tpu as pltpu

F32 = jnp.float32
BF16 = jnp.bfloat16

EPS = 1e-6
N_HEADS = 8
HEAD_DIM = 128
ROT_DIM = 32
ROPE_THETA = 500000.0
MOBA_BLOCK = 256
MOBA_TOPK = 3
PAGE_SIZE = 128
SSM_INNER = 2048
SSM_HEADDIM = 64
SSM_HEADS = 32
SSM_GROUPS = 4
SSM_HPG = SSM_HEADS // SSM_GROUPS
SSM_STATE = 128
SSM_CONV = 4
CONV_DIM = SSM_INNER + 2 * SSM_GROUPS * SSM_STATE
SSD_CHUNK = 128

LANES = 128
SUBLANES = 8
VMEM_LIMIT_BYTES = 56 * 2**20
MOBA_UNROLL = 4
PAGES_PER_STEP = 8
SAMPLE_Q_ROWS = 8

LOG2_E = 1.4426950408889634

_NT = (((1,), (1,)), ((), ()))


def _params(*semantics):
    return pltpu.CompilerParams(dimension_semantics=semantics, vmem_limit_bytes=VMEM_LIMIT_BYTES)


def _rms(x, g):
    return x * lax.rsqrt(jnp.mean(x * x, axis=-1, keepdims=True) + EPS) * g


def _silu(x):
    return x / (1.0 + jnp.exp(-x))


def _qkv_kernel(x_ref, g_ref, w_ref, qg_ref, kg_ref, cos_ref, sa_ref, sb_ref, q_ref, k_ref, v_ref):
    n = _rms(x_ref[...], g_ref[...]).astype(BF16)
    cos, sa, sb = cos_ref[...], sa_ref[...], sb_ref[...]
    d = N_HEADS * HEAD_DIM
    for part, (o_ref, gain_ref) in enumerate(((q_ref, qg_ref), (k_ref, kg_ref), (v_ref, None))):
        y = jnp.dot(n, w_ref[:, part * d:(part + 1) * d], preferred_element_type=F32)
        for h in range(N_HEADS):
            yh = y[:, h * HEAD_DIM:(h + 1) * HEAD_DIM]
            if gain_ref is not None:
                yh = _rms(yh, gain_ref[...])
                yh = (yh * cos + pltpu.roll(yh, HEAD_DIM - ROT_DIM // 2, 1) * sa
                      + pltpu.roll(yh, ROT_DIM // 2, 1) * sb)
            o_ref[h] = yh


def _rope_tables(pos):
    half = ROT_DIM // 2
    inv = ROPE_THETA ** (-jnp.arange(half, dtype=F32) * 2.0 / ROT_DIM)
    ang = pos.astype(F32)[:, None] * inv[None, :]
    cos, sin = jnp.cos(ang), jnp.sin(ang)
    t = pos.shape[0]
    cos_t = jnp.concatenate([cos, cos, jnp.ones((t, HEAD_DIM - ROT_DIM), F32)], axis=1)
    sa = jnp.concatenate([-sin, jnp.zeros((t, HEAD_DIM - half), F32)], axis=1)
    sb = jnp.concatenate([jnp.zeros((t, half), F32), sin, jnp.zeros((t, HEAD_DIM - ROT_DIM), F32)], axis=1)
    return cos_t, sa, sb


def _qkv(x, g, w_bf, qg, kg, pos, tm):
    b, t, d = x.shape
    cos_t, sa, sb = _rope_tables(pos)
    row = lambda v: v.reshape(1, -1)
    out = jax.ShapeDtypeStruct((b, N_HEADS, t, HEAD_DIM), F32)
    tab_spec = pl.BlockSpec((tm, HEAD_DIM), lambda bi, i: (i, 0))
    out_spec = pl.BlockSpec((None, N_HEADS, tm, HEAD_DIM), lambda bi, i: (bi, 0, i, 0))
    const = lambda shape: pl.BlockSpec(shape, lambda bi, i: (0,) * len(shape))
    return pl.pallas_call(
        _qkv_kernel,
        out_shape=(out, out, out),
        grid=(b, t // tm),
        in_specs=[pl.BlockSpec((None, tm, d), lambda bi, i: (bi, i, 0)),
                  const((1, d)), const(w_bf.shape), const((1, HEAD_DIM)), const((1, HEAD_DIM)),
                  tab_spec, tab_spec, tab_spec],
        out_specs=(out_spec, out_spec, out_spec),
        compiler_params=_params("parallel", "parallel"),
        name="qkv_norm_rope",
    )(x, row(g), w_bf, row(qg), row(kg), cos_t, sa, sb)


def _topk_rank(entries, gate, index):
    rank = jnp.zeros(gate.shape, F32)
    for j, c in enumerate(entries):
        ahead = (c > gate) | ((c == gate) & (j < index))
        rank = rank + jnp.where(ahead, 1.0, 0.0)
    return rank


def _moba_prompt_kernel(q_ref, k_ref, v_ref, o_ref, kb_ref, vt_ref, km_ref, bias_ref):
    i = pl.program_id(2)
    nb = k_ref.shape[0] // MOBA_BLOCK
    scale = HEAD_DIM ** -0.5

    @pl.when(i == 0)
    def _():
        kb_ref[...] = k_ref[...].astype(BF16)
        km_ref[...] = jnp.zeros(km_ref.shape, F32)
        for j in range(nb):
            rows = slice(j * MOBA_BLOCK, (j + 1) * MOBA_BLOCK)
            vt_ref[j] = v_ref[rows, :].T.astype(BF16)
            km_ref[j:j + 1, :] = jnp.mean(k_ref[rows, :], axis=0, keepdims=True)

    qf = q_ref[...]
    qb = (qf * (scale * LOG2_E)).astype(BF16)
    gate = lax.dot_general(km_ref[...], qf, _NT, precision=lax.Precision.HIGHEST,
                           preferred_element_type=F32)
    blk = lax.broadcasted_iota(jnp.int32, gate.shape, 0)
    past = blk < i
    gate = jnp.where(past, gate, -jnp.inf)
    rank = _topk_rank([gate[j:j + 1, :] for j in range(nb)], gate, blk)
    bias_ref[...] = jnp.where((rank < MOBA_TOPK) & past, 0.0, -jnp.inf)

    def scores(j):
        start = j * MOBA_BLOCK
        if not isinstance(j, int):
            start = pl.multiple_of(start, MOBA_BLOCK)
        kj = kb_ref[pl.ds(start, MOBA_BLOCK), :]
        return lax.dot_general(kj, qb, _NT, preferred_element_type=F32)

    def attend(n_lead):
        s = scores(i)
        ki = lax.broadcasted_iota(jnp.int32, s.shape, 0)
        qi = lax.broadcasted_iota(jnp.int32, s.shape, 1)
        s_own = jnp.where(ki <= qi, s, -jnp.inf)
        lead = [(j, scores(j), bias_ref[j:j + 1, :]) for j in range(n_lead)]
        m = jnp.max(s_own, axis=0, keepdims=True)
        for _, s, bias in lead:
            m = jnp.maximum(m, jnp.max(s, axis=0, keepdims=True) + bias)
        blocks = [(i, s_own, m)] + [(j, s, m - bias) for j, s, bias in lead]
        l, acc = 0.0, 0.0
        for j, s, shift in blocks:
            p = jnp.exp2(s - shift)
            l = l + jnp.sum(p, axis=0, keepdims=True)
            acc = acc + jnp.dot(vt_ref[j], p.astype(BF16), preferred_element_type=F32)
        o_ref[...] = (acc / l).T.astype(o_ref.dtype)

    groups = pl.cdiv(i, MOBA_UNROLL)
    for v in range(nb // MOBA_UNROLL + 1):
        pl.when(groups == v)(functools.partial(attend, v * MOBA_UNROLL))


def _moba_prompt(q, k, v):
    b, h, s, d = q.shape
    assert s % (MOBA_BLOCK * MOBA_UNROLL) == 0
    tq = MOBA_BLOCK
    nb = s // MOBA_BLOCK
    nb_pad = pl.cdiv(nb, SUBLANES) * SUBLANES
    full = pl.BlockSpec((None, None, s, d), lambda bi, hi, i: (bi, hi, 0, 0))
    return pl.pallas_call(
        _moba_prompt_kernel,
        out_shape=jax.ShapeDtypeStruct((b, s, h * d), BF16),
        grid=(b, h, s // tq),
        in_specs=[pl.BlockSpec((None, None, tq, d), lambda bi, hi, i: (bi, hi, i, 0)), full, full],
        out_specs=pl.BlockSpec((None, tq, d), lambda bi, hi, i: (bi, i, hi)),
        scratch_shapes=[pltpu.VMEM((s, d), BF16),
                        pltpu.VMEM((nb, d, MOBA_BLOCK), BF16),
                        pltpu.VMEM((nb_pad, d), F32),
                        pltpu.VMEM((nb_pad, tq), F32)],
        compiler_params=_params("parallel", "parallel", "arbitrary"),
        name="moba_prompt",
    )(q, k, v)


def _sample_scores_kernel(pt_ref, q_ref, *refs):
    del pt_ref
    k_refs, s_ref = refs[:-1], refs[-1]
    for r, k_ref in enumerate(k_refs):
        for h in range(N_HEADS):
            s = lax.dot_general(q_ref[h].astype(BF16), k_ref[h].astype(BF16), _NT, preferred_element_type=F32)
            s_ref[h, :, r * PAGE_SIZE:(r + 1) * PAGE_SIZE] = s


def _sample_select_kernel(q_ref, kn_ref, vn_ref, s_ref, p_ref, ol_ref, *, n_new, n_full):
    scale = HEAD_DIM ** -0.5
    n_keys = s_ref.shape[-1]
    rows = s_ref.shape[1]
    lane = lax.broadcasted_iota(jnp.int32, (rows, LANES), 1)
    key_blk = (lax.broadcasted_iota(jnp.int32, (rows, n_keys), 1) // MOBA_BLOCK).astype(F32)
    lane_f = lane.astype(F32)
    row = lax.broadcasted_iota(jnp.int32, (rows, 1), 0)
    for h in range(N_HEADS):
        s = s_ref[h]
        cols = [jnp.sum(s[:, j * MOBA_BLOCK:(j + 1) * MOBA_BLOCK], axis=1, keepdims=True) * (1.0 / MOBA_BLOCK)
                for j in range(n_full)]
        gate = jnp.full((rows, LANES), -jnp.inf, F32)
        for j, c in enumerate(cols):
            gate = jnp.where(lane == j, c, gate)
        rank = _topk_rank(cols, gate, lane)
        mask = jnp.zeros(key_blk.shape, jnp.bool_)
        for r in range(min(MOBA_TOPK, n_full)):
            pick = jnp.sum(jnp.where((rank == r) & (lane < n_full), lane_f, 0.0), axis=1, keepdims=True)
            mask = mask | (key_blk == pick)
        s = jnp.where(mask, s * scale, -jnp.inf)
        q = q_ref[h]
        s_new = []
        for u in range(n_new):
            su = jnp.sum(q * kn_ref[h, u:u + 1, :], axis=1, keepdims=True) * scale
            s_new.append(jnp.where(row >= u, su, -jnp.inf))
        m = jnp.max(s, axis=1, keepdims=True)
        for su in s_new:
            m = jnp.maximum(m, su)
        p = jnp.exp(s - m)
        p_new = [jnp.exp(su - m) for su in s_new]
        denom = jnp.sum(p, axis=1, keepdims=True)
        for pu in p_new:
            denom = denom + pu
        p_ref[h] = (p / denom).astype(p_ref.dtype)
        o = jnp.zeros((rows, HEAD_DIM), F32)
        for u, pu in enumerate(p_new):
            o = o + (pu / denom) * vn_ref[h, u:u + 1, :]
        ol_ref[h] = o


def _sample_pv_kernel(pt_ref, p_ref, ol_ref, *refs):
    del pt_ref
    v_refs, o_ref = refs[:-1], refs[-1]

    @pl.when(pl.program_id(1) == 0)
    def _():
        o_ref[...] = ol_ref[...]

    for r, v_ref in enumerate(v_refs):
        for h in range(N_HEADS):
            p = p_ref[h, :, r * PAGE_SIZE:(r + 1) * PAGE_SIZE].astype(BF16)
            o_ref[h] += jnp.dot(p, v_ref[h].astype(BF16), preferred_element_type=F32)


def _page_specs(n):
    def spec(r):
        return pl.BlockSpec((None, N_HEADS, PAGE_SIZE, HEAD_DIM),
                            lambda bi, g, pt: (pt[bi, g * n + r], 0, 0, 0))
    return [spec(r) for r in range(n)]


def _moba_sample(q, k_new, v_new, cache_k, cache_v, page_table):
    db, h, t, d = q.shape
    n_pages = page_table.shape[1]
    assert (n_pages * PAGE_SIZE) % MOBA_BLOCK == 0, "cached keys must end on a MoBA block boundary"
    assert n_pages % PAGES_PER_STEP == 0 and t <= SAMPLE_Q_ROWS
    n_full = n_pages * PAGE_SIZE // MOBA_BLOCK
    n_keys = n_pages * PAGE_SIZE
    rows = SAMPLE_Q_ROWS
    steps = n_pages // PAGES_PER_STEP
    pad = lambda a: jnp.pad(a, ((0, 0), (0, 0), (0, rows - t), (0, 0)))
    qp, knp, vnp = pad(q), pad(k_new), pad(v_new)
    per_seq = pl.BlockSpec((None, h, rows, d), lambda bi, g, pt: (bi, 0, 0, 0))
    keys_spec = pl.BlockSpec((None, h, rows, PAGES_PER_STEP * PAGE_SIZE), lambda bi, g, pt: (bi, 0, 0, g))

    scores = pl.pallas_call(
        _sample_scores_kernel,
        out_shape=jax.ShapeDtypeStruct((db, h, rows, n_keys), F32),
        grid_spec=pltpu.PrefetchScalarGridSpec(
            num_scalar_prefetch=1, grid=(db, steps),
            in_specs=[per_seq] + _page_specs(PAGES_PER_STEP), out_specs=keys_spec),
        compiler_params=_params("parallel", "arbitrary"),
        name="moba_sample_scores",
    )(page_table, qp, *([cache_k] * PAGES_PER_STEP))

    seq4 = lambda last: pl.BlockSpec((None, h, rows, last), lambda bi: (bi, 0, 0, 0))
    probs, o_new = pl.pallas_call(
        functools.partial(_sample_select_kernel, n_new=t, n_full=n_full),
        out_shape=(jax.ShapeDtypeStruct((db, h, rows, n_keys), F32),
                   jax.ShapeDtypeStruct((db, h, rows, d), F32)),
        grid=(db,),
        in_specs=[seq4(d), seq4(d), seq4(d), seq4(n_keys)],
        out_specs=(seq4(n_keys), seq4(d)),
        compiler_params=_params("parallel"),
        name="moba_sample_select",
    )(qp, knp, vnp, scores)

    out = pl.pallas_call(
        _sample_pv_kernel,
        out_shape=jax.ShapeDtypeStruct((db, h, rows, d), F32),
        grid_spec=pltpu.PrefetchScalarGridSpec(
            num_scalar_prefetch=1, grid=(db, steps),
            in_specs=[keys_spec, per_seq] + _page_specs(PAGES_PER_STEP), out_specs=per_seq),
        compiler_params=_params("parallel", "arbitrary"),
        name="moba_sample_pv",
    )(page_table, probs, o_new, *([cache_v] * PAGES_PER_STEP))
    return jnp.transpose(out[:, :, :t], (0, 2, 1, 3)).reshape(db, t, h * d).astype(BF16)


def _proj_ffn_kernel(a_ref, x_ref, wp_ref, g_ref, wg_ref, wu_ref, wd_ref, o_ref, *, chunks):
    hid = x_ref[...] + jnp.dot(a_ref[...], wp_ref[...], preferred_element_type=F32)
    n = _rms(hid, g_ref[...]).astype(BF16)
    width = wg_ref.shape[1] // chunks
    acc = hid
    for c in range(chunks):
        cols = slice(c * width, (c + 1) * width)
        gate = jnp.dot(n, wg_ref[:, cols], preferred_element_type=F32)
        up = jnp.dot(n, wu_ref[:, cols], preferred_element_type=F32)
        act = (_silu(gate) * up).astype(BF16)
        acc = acc + jnp.dot(act, wd_ref[cols, :], preferred_element_type=F32)
    o_ref[...] = acc


def _proj_ffn(a_bf, x, wp_bf, g, wg_bf, wu_bf, wd_bf, tm, chunks=2):
    m, d = x.shape
    ka = a_bf.shape[1]
    const = lambda shape: pl.BlockSpec(shape, lambda i: (0, 0), pipeline_mode=pl.Buffered(1))
    return pl.pallas_call(
        functools.partial(_proj_ffn_kernel, chunks=chunks),
        out_shape=jax.ShapeDtypeStruct((m, d), F32),
        grid=(m // tm,),
        in_specs=[pl.BlockSpec((tm, ka), lambda i: (i, 0)), pl.BlockSpec((tm, d), lambda i: (i, 0)),
                  const(wp_bf.shape), const((1, d)), const(wg_bf.shape), const(wu_bf.shape), const(wd_bf.shape)],
        out_specs=pl.BlockSpec((tm, d), lambda i: (i, 0)),
        compiler_params=_params("parallel"),
        name="proj_ffn",
    )(a_bf, x, wp_bf, g.reshape(1, d), wg_bf, wu_bf, wd_bf)


def _inproj_kernel(x_ref, g_ref, wz_ref, wx_ref, wdt_ref, z_ref, xbc_ref, dt_ref):
    n = _rms(x_ref[...], g_ref[...]).astype(BF16)
    z_ref[...] = jnp.dot(n, wz_ref[...], preferred_element_type=F32)
    xbc_ref[...] = jnp.dot(n, wx_ref[...], preferred_element_type=F32)
    dt_ref[...] = jnp.dot(n, wdt_ref[...], preferred_element_type=F32)


def _inproj(x, g, wz_bf, wx_bf, wdt_bf, tm):
    m, d = x.shape
    const = lambda shape: pl.BlockSpec(shape, lambda i: (0, 0), pipeline_mode=pl.Buffered(1))
    tile = lambda n: pl.BlockSpec((tm, n), lambda i: (i, 0))
    return pl.pallas_call(
        _inproj_kernel,
        out_shape=(jax.ShapeDtypeStruct((m, SSM_INNER), F32), jax.ShapeDtypeStruct((m, CONV_DIM), F32),
                   jax.ShapeDtypeStruct((m, LANES), F32)),
        grid=(m // tm,),
        in_specs=[tile(d), const((1, d)), const(wz_bf.shape), const(wx_bf.shape), const(wdt_bf.shape)],
        out_specs=(tile(SSM_INNER), tile(CONV_DIM), tile(LANES)),
        compiler_params=_params("parallel"),
        name="ssm_in_proj",
    )(x, g.reshape(1, d), wz_bf, wx_bf, wdt_bf)


def _ssd_kernel(*refs, rows, has_init):
    if has_init:
        (z_ref, xbc_ref, dt_ref, cw_ref, cb_ref, dtb_ref, alog_ref, dsk_ref, gn_ref, conv0_ref, ssm0_ref,
         y_ref, hout_ref, buf_ref, act_ref, ysc_ref, ht_ref) = refs
    else:
        (z_ref, xbc_ref, dt_ref, cw_ref, cb_ref, dtb_ref, alog_ref, dsk_ref, gn_ref,
         y_ref, hout_ref, buf_ref, act_ref, ysc_ref, ht_ref) = refs
    L = SSD_CHUNK
    P, N, G = SSM_HEADDIM, SSM_STATE, SSM_GROUPS
    PAIRS = SSM_HEADS // 2
    PPG = PAIRS // G
    GW = SSM_HPG * P
    assert L == N == 2 * P == LANES
    c = pl.program_id(1)
    last = pl.num_programs(1) - 1
    tail = SUBLANES - (SSM_CONV - 1)

    @pl.when(c == 0)
    def _():
        buf_ref[...] = jnp.zeros(buf_ref.shape, F32)
        if has_init:
            buf_ref[0:SUBLANES, :] = conv0_ref[...]
            for pair in range(PAIRS):
                two = jnp.concatenate([ssm0_ref[2 * pair], ssm0_ref[2 * pair + 1]], axis=0)
                ht_ref[pair] = two.T
        else:
            ht_ref[...] = jnp.zeros(ht_ref.shape, F32)

    buf_ref[SUBLANES:SUBLANES + rows, :] = xbc_ref[...]
    conv = cb_ref[...] + buf_ref[tail:tail + L, :] * cw_ref[0:1, :]
    for w in range(1, SSM_CONV):
        conv = conv + buf_ref[tail + w:tail + w + L, :] * cw_ref[w:w + 1, :]
    act_ref[...] = _silu(conv)
    if rows == L:
        buf_ref[0:SUBLANES, :] = buf_ref[L:L + SUBLANES, :]

    x = dt_ref[...] + dtb_ref[...]
    dt = jnp.maximum(x, 0.0) + jnp.log1p(jnp.exp(-jnp.abs(x)))
    if rows < L:
        dt = jnp.where(lax.broadcasted_iota(jnp.int32, dt.shape, 0) < rows, dt, 0.0)
    a = dt * (-jnp.exp(alog_ref[...]))
    ti = lax.broadcasted_iota(jnp.int32, (L, L), 0)
    si = lax.broadcasted_iota(jnp.int32, (L, L), 1)
    causal = ti >= si
    a_cs = jnp.dot(jnp.where(causal, 1.0, 0.0), a, precision=lax.Precision.HIGHEST,
                   preferred_element_type=F32)
    a_tot = a_cs[L - 1:L, :]
    ea_tot = jnp.exp(a_tot)
    a_cs_t = a_cs.T
    dt_t = dt.T
    dtde_t = (dt * jnp.exp(a_tot - a_cs)).T
    first = lax.broadcasted_iota(jnp.int32, (L, LANES), 1) < P

    for g in range(G):
        bg = act_ref[:, SSM_INNER + g * N:SSM_INNER + (g + 1) * N]
        cg = act_ref[:, SSM_INNER + G * N + g * N:SSM_INNER + G * N + (g + 1) * N]
        cb = lax.dot_general(cg.astype(BF16), bg.astype(BF16), _NT, preferred_element_type=F32)
        bg_t = bg.T
        for pr in range(PPG):
            pair = g * PPG + pr
            lanes = slice(pair * LANES, (pair + 1) * LANES)
            xs = act_ref[:, lanes]
            ht = ht_ref[pair]
            y = dsk_ref[:, lanes] * xs
            upd = jnp.zeros((N, LANES), F32)
            for second in range(2):
                e = 2 * pair + second
                mine = lambda v: (jnp.where(first, 0.0, v) if second else jnp.where(first, v, 0.0)).astype(BF16)
                xs_e, ht_e = mine(xs), mine(ht)
                acs = jnp.broadcast_to(a_cs[:, e:e + 1], (L, L))
                decay = jnp.exp(jnp.where(causal, acs - a_cs_t[e:e + 1, :], -jnp.inf))
                w = (cb * decay * dt_t[e:e + 1, :]).astype(BF16)
                c_in = (cg * jnp.exp(acs)).astype(BF16)
                y = (y + jnp.dot(w, xs_e, preferred_element_type=F32)
                     + jnp.dot(c_in, ht_e, preferred_element_type=F32))
                b_out = (bg_t * dtde_t[e:e + 1, :]).astype(BF16)
                upd = upd + jnp.dot(b_out, xs_e, preferred_element_type=F32)
            ysc_ref[:, lanes] = y
            keep = jnp.where(first[0:1, :], ea_tot[:, 2 * pair:2 * pair + 1], ea_tot[:, 2 * pair + 1:2 * pair + 2])
            ht_ref[pair] = keep * ht + upd

    z_in = z_ref[...]
    for g in range(G):
        lanes = slice(g * GW, (g + 1) * GW)
        yg = ysc_ref[0:rows, lanes] * _silu(z_in[:, lanes])
        y_ref[:, lanes] = _rms(yg, gn_ref[:, lanes]).astype(y_ref.dtype)

    @pl.when(c == last)
    def _():
        for pair in range(PAIRS):
            two = ht_ref[pair].T
            hout_ref[2 * pair] = two[0:P]
            hout_ref[2 * pair + 1] = two[P:2 * P]


def _ssd(z, xbc, dt, conv_w, conv_b, dt_bias, a_log, d_skip, gate_norm, conv0=None, ssm0=None):
    b, t, _ = z.shape
    has_init = conv0 is not None
    rows = min(SSD_CHUNK, t)
    assert t % rows == 0 and (rows == SSD_CHUNK or t == rows)
    nc = t // rows
    if rows < SSD_CHUNK:
        dt = jnp.pad(dt, ((0, 0), (0, SSD_CHUNK - rows), (0, 0)))
    lane_pad = lambda v: jnp.pad(v.astype(F32), (0, LANES - v.shape[0])).reshape(1, LANES)
    tile = lambda n: pl.BlockSpec((None, rows, n), lambda bi, ci: (bi, ci, 0))
    const = lambda shape: pl.BlockSpec(shape, lambda bi, ci: (0,) * len(shape))
    state_spec = pl.BlockSpec((None, SSM_HEADS, SSM_HEADDIM, SSM_STATE), lambda bi, ci: (bi, 0, 0, 0))
    in_specs = [tile(SSM_INNER), tile(CONV_DIM), pl.BlockSpec((None, SSD_CHUNK, LANES), lambda bi, ci: (bi, ci, 0)),
                const((SSM_CONV, CONV_DIM)), const((1, CONV_DIM)), const((1, LANES)), const((1, LANES)),
                const((1, SSM_INNER)), const((1, SSM_INNER))]
    args = [z, xbc, dt, conv_w, conv_b.reshape(1, CONV_DIM), lane_pad(dt_bias), lane_pad(a_log),
            jnp.repeat(d_skip.astype(F32), SSM_HEADDIM).reshape(1, SSM_INNER), gate_norm.reshape(1, SSM_INNER)]
    if has_init:
        in_specs += [pl.BlockSpec((None, SUBLANES, CONV_DIM), lambda bi, ci: (bi, 0, 0)), state_spec]
        args += [jnp.pad(conv0, ((0, 0), (SUBLANES - (SSM_CONV - 1), 0), (0, 0))), ssm0]
    y_dtype = BF16 if rows % 16 == 0 else F32
    return pl.pallas_call(
        functools.partial(_ssd_kernel, rows=rows, has_init=has_init),
        out_shape=(jax.ShapeDtypeStruct((b, t, SSM_INNER), y_dtype),
                   jax.ShapeDtypeStruct((b, SSM_HEADS, SSM_HEADDIM, SSM_STATE), F32)),
        grid=(b, nc),
        in_specs=in_specs,
        out_specs=(tile(SSM_INNER), state_spec),
        scratch_shapes=[pltpu.VMEM((SUBLANES + SSD_CHUNK, CONV_DIM), F32),
                        pltpu.VMEM((SSD_CHUNK, CONV_DIM), F32),
                        pltpu.VMEM((SSD_CHUNK, SSM_INNER), F32),
                        pltpu.VMEM((SSM_HEADS // 2, SSM_STATE, 2 * SSM_HEADDIM), F32)],
        compiler_params=_params("parallel", "arbitrary"),
        name="ssd_chunk",
    )(*args)


def kernel(x_prompt, x_sample, cache_k, cache_v, page_table, state_conv, state_ssm, attn_norm, w_qkv, q_norm, k_norm, w_o, ssm_norm, w_in, conv_w, conv_b, dt_bias, a_log, d_skip, gate_norm, w_out, ffn_norm, w_gate, w_up, w_down):
    b, s, d = x_prompt.shape
    db, t, _ = x_sample.shape
    past = page_table.shape[1] * PAGE_SIZE
    tm_p, tm_s = 512, db * t
    bf = lambda w: w.astype(BF16)

    w_qkv_bf, w_o_bf = bf(w_qkv[0]), bf(w_o[0])
    ffn0 = (ffn_norm[0], bf(w_gate[0]), bf(w_up[0]), bf(w_down[0]))
    q_p, k_p, v_p = _qkv(x_prompt, attn_norm[0], w_qkv_bf, q_norm[0], k_norm[0], jnp.arange(s), tm_p)
    attn_p = _moba_prompt(q_p, k_p, v_p)
    pos_s = past + jnp.tile(jnp.arange(t), db)
    xs_rows = x_sample.reshape(1, db * t, d)
    qkv_s = _qkv(xs_rows, attn_norm[0], w_qkv_bf, q_norm[0], k_norm[0], pos_s, tm_s)
    q_s, k_s, v_s = (jnp.transpose(a.reshape(N_HEADS, db, t, HEAD_DIM), (1, 0, 2, 3)) for a in qkv_s)
    attn_s = _moba_sample(q_s, k_s, v_s, cache_k[0], cache_v[0], page_table)
    h_p = _proj_ffn(attn_p.reshape(b * s, d), x_prompt.reshape(b * s, d), w_o_bf, *ffn0, tm_p)
    h_s = _proj_ffn(attn_s.reshape(db * t, d), x_sample.reshape(db * t, d), w_o_bf, *ffn0, tm_s)

    w_in1 = w_in[0]
    wz_bf = bf(w_in1[:, :SSM_INNER])
    wx_bf = bf(w_in1[:, SSM_INNER:SSM_INNER + CONV_DIM])
    wdt_bf = bf(jnp.pad(w_in1[:, SSM_INNER + CONV_DIM:], ((0, 0), (0, LANES - SSM_HEADS))))
    w_out_bf = bf(w_out[0])
    ffn1 = (ffn_norm[1], bf(w_gate[1]), bf(w_up[1]), bf(w_down[1]))
    ssd_w = (conv_w[0], conv_b[0], dt_bias[0], a_log[0], d_skip[0], gate_norm[0])
    z_p, xbc_p, dt_p = _inproj(h_p, ssm_norm[0], wz_bf, wx_bf, wdt_bf, 256)
    z_s, xbc_s, dt_s = _inproj(h_s, ssm_norm[0], wz_bf, wx_bf, wdt_bf, tm_s)
    xbc_p = xbc_p.reshape(b, s, CONV_DIM)
    xbc_s = xbc_s.reshape(db, t, CONV_DIM)
    y_p, ssm_p = _ssd(z_p.reshape(b, s, SSM_INNER), xbc_p, dt_p.reshape(b, s, LANES), *ssd_w)
    y_s, ssm_s = _ssd(z_s.reshape(db, t, SSM_INNER), xbc_s, dt_s.reshape(db, t, LANES), *ssd_w,
                      conv0=state_conv[0], ssm0=state_ssm[0])
    out_p = _proj_ffn(y_p.reshape(b * s, SSM_INNER), h_p, w_out_bf, *ffn1, tm_p)
    out_s = _proj_ffn(bf(y_s).reshape(db * t, SSM_INNER), h_s, w_out_bf, *ffn1, tm_s)

    keep = SSM_CONV - 1
    conv_p = xbc_p[:, s - keep:]
    conv_s = jnp.concatenate([state_conv[0], xbc_s], axis=1)[:, t:]
    return (out_p.reshape(b, s, d), out_s.reshape(db, t, d), k_p[None], v_p[None], k_s[None], v_s[None],
            conv_p[None], ssm_p[None], conv_s[None], ssm_s[None])
```

```python
import functools

import jax
import jax.numpy as jnp
from jax import lax
from jax.experimental import pallas as pl
from jax.experimental.pallas import tpu as pltpu

F32 = jnp.float32
BF16 = jnp.bfloat16

EPS = 1e-6
N_HEADS = 8
HEAD_DIM = 128
ROT_DIM = 32
ROPE_THETA = 500000.0
MOBA_BLOCK = 256
MOBA_TOPK = 3
PAGE_SIZE = 128
SSM_INNER = 2048
SSM_HEADDIM = 64
SSM_HEADS = 32
SSM_GROUPS = 4
SSM_HPG = SSM_HEADS // SSM_GROUPS
SSM_STATE = 128
SSM_CONV = 4
CONV_DIM = SSM_INNER + 2 * SSM_GROUPS * SSM_STATE
SSD_CHUNK = 128

LANES = 128
SUBLANES = 8
VMEM_LIMIT_BYTES = 56 * 2**20
MOBA_UNROLL = 4
MOBA_HEADS_PER_STEP = 4
PAGES_PER_STEP = 16
SAMPLE_Q_ROWS = 8

LOG2_E = 1.4426950408889634

_NT = (((1,), (1,)), ((), ()))


def _params(*semantics):
    return pltpu.CompilerParams(dimension_semantics=semantics, vmem_limit_bytes=VMEM_LIMIT_BYTES)


def _rms(x, g):
    return x * lax.rsqrt(jnp.mean(x * x, axis=-1, keepdims=True) + EPS) * g


def _silu(x):
    return x / (1.0 + jnp.exp(-x))


def _qkv_kernel(x_ref, g_ref, w_ref, qg_ref, kg_ref, cos_ref, sa_ref, sb_ref, q_ref, k_ref, v_ref):
    n = _rms(x_ref[...], g_ref[...]).astype(BF16)
    cos, sa, sb = cos_ref[...], sa_ref[...], sb_ref[...]
    d = N_HEADS * HEAD_DIM
    for part, (o_ref, gain_ref) in enumerate(((q_ref, qg_ref), (k_ref, kg_ref), (v_ref, None))):
        y = jnp.dot(n, w_ref[:, part * d:(part + 1) * d], preferred_element_type=F32)
        for h in range(N_HEADS):
            yh = y[:, h * HEAD_DIM:(h + 1) * HEAD_DIM]
            if gain_ref is not None:
                yh = _rms(yh, gain_ref[...])
                yh = (yh * cos + pltpu.roll(yh, HEAD_DIM - ROT_DIM // 2, 1) * sa
                      + pltpu.roll(yh, ROT_DIM // 2, 1) * sb)
            o_ref[h] = yh


def _rope_tables(pos):
    half = ROT_DIM // 2
    inv = ROPE_THETA ** (-jnp.arange(half, dtype=F32) * 2.0 / ROT_DIM)
    ang = pos.astype(F32)[:, None] * inv[None, :]
    cos, sin = jnp.cos(ang), jnp.sin(ang)
    t = pos.shape[0]
    cos_t = jnp.concatenate([cos, cos, jnp.ones((t, HEAD_DIM - ROT_DIM), F32)], axis=1)
    sa = jnp.concatenate([-sin, jnp.zeros((t, HEAD_DIM - half), F32)], axis=1)
    sb = jnp.concatenate([jnp.zeros((t, half), F32), sin, jnp.zeros((t, HEAD_DIM - ROT_DIM), F32)], axis=1)
    return cos_t, sa, sb


def _qkv(x, g, w_bf, qg, kg, pos, tm):
    b, t, d = x.shape
    cos_t, sa, sb = _rope_tables(pos)
    row = lambda v: v.reshape(1, -1)
    out = jax.ShapeDtypeStruct((b, N_HEADS, t, HEAD_DIM), F32)
    tab_spec = pl.BlockSpec((tm, HEAD_DIM), lambda bi, i: (i, 0))
    out_spec = pl.BlockSpec((None, N_HEADS, tm, HEAD_DIM), lambda bi, i: (bi, 0, i, 0))
    const = lambda shape: pl.BlockSpec(shape, lambda bi, i: (0,) * len(shape))
    return pl.pallas_call(
        _qkv_kernel,
        out_shape=(out, out, out),
        grid=(b, t // tm),
        in_specs=[pl.BlockSpec((None, tm, d), lambda bi, i: (bi, i, 0)),
                  const((1, d)), const(w_bf.shape), const((1, HEAD_DIM)), const((1, HEAD_DIM)),
                  tab_spec, tab_spec, tab_spec],
        out_specs=(out_spec, out_spec, out_spec),
        compiler_params=_params("parallel", "parallel"),
        name="qkv_norm_rope",
    )(x, row(g), w_bf, row(qg), row(kg), cos_t, sa, sb)


def _topk_rank(entries, gate, index):
    rank = jnp.zeros(gate.shape, F32)
    for j, c in enumerate(entries):
        ahead = (c > gate) | ((c == gate) & (j < index))
        rank = rank + jnp.where(ahead, 1.0, 0.0)
    return rank


def _moba_prompt_kernel(q_ref, k_ref, v_ref, o_ref, kb_ref, vt_ref, km_ref, bias_ref):
    i = pl.program_id(2)
    heads = q_ref.shape[0]
    nb = k_ref.shape[1] // MOBA_BLOCK
    scale = HEAD_DIM ** -0.5

    @pl.when(i == 0)
    def _():
        kb_ref[...] = k_ref[...].astype(BF16)
        km_ref[...] = jnp.zeros(km_ref.shape, F32)
        for h in range(heads):
            for j in range(nb):
                rows = slice(j * MOBA_BLOCK, (j + 1) * MOBA_BLOCK)
                vt_ref[h, j] = v_ref[h, rows, :].T.astype(BF16)
                km_ref[h, j:j + 1, :] = jnp.mean(k_ref[h, rows, :], axis=0, keepdims=True)

    qbs = []
    for h in range(heads):
        qf = q_ref[h]
        qbs.append((qf * (scale * LOG2_E)).astype(BF16))
        gate = lax.dot_general(km_ref[h], qf, _NT, precision=lax.Precision.HIGHEST,
                               preferred_element_type=F32)
        blk = lax.broadcasted_iota(jnp.int32, gate.shape, 0)
        past = blk < i
        gate = jnp.where(past, gate, -jnp.inf)
        rank = _topk_rank([gate[j:j + 1, :] for j in range(nb)], gate, blk)
        bias_ref[h] = jnp.where((rank < MOBA_TOPK) & past, 0.0, -jnp.inf)

    def scores(h, j):
        start = j * MOBA_BLOCK
        if not isinstance(j, int):
            start = pl.multiple_of(start, MOBA_BLOCK)
        kj = kb_ref[h, pl.ds(start, MOBA_BLOCK), :]
        return lax.dot_general(kj, qbs[h], _NT, preferred_element_type=F32)

    def attend(n_lead):
        for h in range(heads):
            s = scores(h, i)
            ki = lax.broadcasted_iota(jnp.int32, s.shape, 0)
            qi = lax.broadcasted_iota(jnp.int32, s.shape, 1)
            s_own = jnp.where(ki <= qi, s, -jnp.inf)
            lead = [(j, scores(h, j), bias_ref[h, j:j + 1, :]) for j in range(n_lead)]
            m = jnp.max(s_own, axis=0, keepdims=True)
            for _, s, bias in lead:
                m = jnp.maximum(m, jnp.max(s, axis=0, keepdims=True) + bias)
            blocks = [(i, s_own, m)] + [(j, s, m - bias) for j, s, bias in lead]
            l, acc = 0.0, 0.0
            for j, s, shift in blocks:
                p = jnp.exp2(s - shift)
                l = l + jnp.sum(p, axis=0, keepdims=True)
                acc = acc + jnp.dot(vt_ref[h, j], p.astype(BF16), preferred_element_type=F32)
            o_ref[:, h * HEAD_DIM:(h + 1) * HEAD_DIM] = (acc / l).T.astype(o_ref.dtype)

    groups = pl.cdiv(i, MOBA_UNROLL)
    for v in range(nb // MOBA_UNROLL + 1):
        pl.when(groups == v)(functools.partial(attend, v * MOBA_UNROLL))


def _moba_prompt(q, k, v):
    b, h, s, d = q.shape
    hs = MOBA_HEADS_PER_STEP
    assert s % (MOBA_BLOCK * MOBA_UNROLL) == 0 and h % hs == 0
    tq = MOBA_BLOCK
    nb = s // MOBA_BLOCK
    nb_pad = pl.cdiv(nb, SUBLANES) * SUBLANES
    full = pl.BlockSpec((None, hs, s, d), lambda bi, hi, i: (bi, hi, 0, 0))
    return pl.pallas_call(
        _moba_prompt_kernel,
        out_shape=jax.ShapeDtypeStruct((b, s, h * d), BF16),
        grid=(b, h // hs, s // tq),
        in_specs=[pl.BlockSpec((None, hs, tq, d), lambda bi, hi, i: (bi, hi, i, 0)), full, full],
        out_specs=pl.BlockSpec((None, tq, hs * d), lambda bi, hi, i: (bi, i, hi)),
        scratch_shapes=[pltpu.VMEM((hs, s, d), BF16),
                        pltpu.VMEM((hs, nb, d, MOBA_BLOCK), BF16),
                        pltpu.VMEM((hs, nb_pad, d), F32),
                        pltpu.VMEM((hs, nb_pad, tq), F32)],
        compiler_params=_params("parallel", "parallel", "arbitrary"),
        name="moba_prompt",
    )(q, k, v)


def _sample_scores_kernel(pt_ref, q_ref, *refs):
    del pt_ref
    k_refs, s_ref = refs[:-1], refs[-1]
    for r, k_ref in enumerate(k_refs):
        for h in range(N_HEADS):
            s = lax.dot_general(q_ref[h].astype(BF16), k_ref[h].astype(BF16), _NT, preferred_element_type=F32)
            s_ref[h, :, r * PAGE_SIZE:(r + 1) * PAGE_SIZE] = s


def _sample_select_kernel(q_ref, kn_ref, vn_ref, s_ref, pc_ref, sel_ref, ol_ref, *, n_new, n_full):
    scale = HEAD_DIM ** -0.5
    rows = s_ref.shape[1]
    n_top = min(MOBA_TOPK, n_full)
    lane = lax.broadcasted_iota(jnp.int32, (rows, LANES), 1)
    lane_f = lane.astype(F32)
    row = lax.broadcasted_iota(jnp.int32, (rows, 1), 0)
    for h in range(N_HEADS):
        s = s_ref[h]
        blocks = [s[:, j * MOBA_BLOCK:(j + 1) * MOBA_BLOCK] for j in range(n_full)]
        cols = [jnp.sum(blk, axis=1, keepdims=True) * (1.0 / MOBA_BLOCK) for blk in blocks]
        gate = jnp.full((rows, LANES), -jnp.inf, F32)
        for j, c in enumerate(cols):
            gate = jnp.where(lane == j, c, gate)
        rank = _topk_rank(cols, gate, lane)
        picks = [jnp.sum(jnp.where((rank == r) & (lane < n_full), lane_f, 0.0), axis=1, keepdims=True)
                 for r in range(n_top)]
        s_sel = []
        for pick in picks:
            g = jnp.zeros((rows, MOBA_BLOCK), F32)
            for j, blk in enumerate(blocks):
                g = g + jnp.where(pick == j, blk, 0.0)
            s_sel.append(g * scale)
        q = q_ref[h]
        s_new = []
        for u in range(n_new):
            su = jnp.sum(q * kn_ref[h, u:u + 1, :], axis=1, keepdims=True) * scale
            s_new.append(jnp.where(row >= u, su, -jnp.inf))
        m = functools.reduce(jnp.maximum, [jnp.max(g, axis=1, keepdims=True) for g in s_sel] + s_new)
        p_sel = [jnp.exp(g - m) for g in s_sel]
        p_new = [jnp.exp(su - m) for su in s_new]
        denom = functools.reduce(jnp.add, [jnp.sum(p, axis=1, keepdims=True) for p in p_sel] + p_new)
        ids = jnp.zeros((rows, LANES), F32)
        for r in range(MOBA_TOPK):
            cols_r = slice(r * MOBA_BLOCK, (r + 1) * MOBA_BLOCK)
            if r < n_top:
                pc_ref[h, :, cols_r] = p_sel[r] / denom
                ids = jnp.where(lane == r, picks[r], ids)
            else:
                pc_ref[h, :, cols_r] = jnp.zeros((rows, MOBA_BLOCK), F32)
        sel_ref[h] = ids.astype(jnp.int32)
        o = jnp.zeros((rows, HEAD_DIM), F32)
        for u, pu in enumerate(p_new):
            o = o + (pu / denom) * vn_ref[h, u:u + 1, :]
        ol_ref[h] = o


def _sample_pv_kernel(pt_ref, sel_ref, pc_ref, ol_ref, v_hbm, o_ref, vbuf, sems, *, n_new):
    b = pl.program_id(0)
    n_seq = pl.num_programs(0)
    rows = pc_ref.shape[1]
    ppb = MOBA_BLOCK // PAGE_SIZE
    row = lax.broadcasted_iota(jnp.int32, (rows, MOBA_BLOCK), 0)

    def copies(seq, h):
        out = []
        for t in range(n_new):
            for r in range(MOBA_TOPK):
                k = t * MOBA_TOPK + r
                blk = sel_ref[seq, (h * n_new + t) * MOBA_TOPK + r]
                for half in range(ppb):
                    page = pt_ref[seq, blk * ppb + half]
                    out.append(pltpu.make_async_copy(
                        v_hbm.at[page, h], vbuf.at[h % 2, k, pl.ds(half * PAGE_SIZE, PAGE_SIZE), :], sems.at[h % 2]))
        return out

    @pl.when(b == 0)
    def _():
        for c in copies(b, 0):
            c.start()

    for h in range(N_HEADS):
        if h + 1 < N_HEADS:
            for c in copies(b, h + 1):
                c.start()
        else:
            @pl.when(b + 1 < n_seq)
            def _():
                for c in copies(b + 1, 0):
                    c.start()
        for c in copies(b, h):
            c.wait()
        acc = ol_ref[h]
        for t in range(n_new):
            for r in range(MOBA_TOPK):
                p = jnp.where(row == t, pc_ref[h, :, r * MOBA_BLOCK:(r + 1) * MOBA_BLOCK], 0.0).astype(BF16)
                acc = acc + jnp.dot(p, vbuf[h % 2, t * MOBA_TOPK + r].astype(BF16), preferred_element_type=F32)
        o_ref[h] = acc


def _page_specs(n):
    def spec(r):
        return pl.BlockSpec((None, N_HEADS, PAGE_SIZE, HEAD_DIM),
                            lambda bi, g, pt: (pt[bi, g * n + r], 0, 0, 0))
    return [spec(r) for r in range(n)]


def _moba_sample(q, k_new, v_new, cache_k, cache_v, page_table):
    db, h, t, d = q.shape
    n_pages = page_table.shape[1]
    assert (n_pages * PAGE_SIZE) % MOBA_BLOCK == 0, "cached keys must end on a MoBA block boundary"
    assert n_pages % PAGES_PER_STEP == 0 and t <= SAMPLE_Q_ROWS and h % 2 == 0
    n_full = n_pages * PAGE_SIZE // MOBA_BLOCK
    n_keys = n_pages * PAGE_SIZE
    rows = SAMPLE_Q_ROWS
    steps = n_pages // PAGES_PER_STEP
    pad = lambda a: jnp.pad(a, ((0, 0), (0, 0), (0, rows - t), (0, 0)))
    qp, knp, vnp = pad(q), pad(k_new), pad(v_new)
    per_seq = pl.BlockSpec((None, h, rows, d), lambda bi, g, pt: (bi, 0, 0, 0))
    keys_spec = pl.BlockSpec((None, h, rows, PAGES_PER_STEP * PAGE_SIZE), lambda bi, g, pt: (bi, 0, 0, g))

    scores = pl.pallas_call(
        _sample_scores_kernel,
        out_shape=jax.ShapeDtypeStruct((db, h, rows, n_keys), F32),
        grid_spec=pltpu.PrefetchScalarGridSpec(
            num_scalar_prefetch=1, grid=(db, steps),
            in_specs=[per_seq] + _page_specs(PAGES_PER_STEP), out_specs=keys_spec),
        compiler_params=_params("parallel", "arbitrary"),
        name="moba_sample_scores",
    )(page_table, qp, *([cache_k] * PAGES_PER_STEP))

    seq4 = lambda last: pl.BlockSpec((None, h, rows, last), lambda bi: (bi, 0, 0, 0))
    probs, sel, o_new = pl.pallas_call(
        functools.partial(_sample_select_kernel, n_new=t, n_full=n_full),
        out_shape=(jax.ShapeDtypeStruct((db, h, rows, MOBA_TOPK * MOBA_BLOCK), F32),
                   jax.ShapeDtypeStruct((db, h, rows, LANES), jnp.int32),
                   jax.ShapeDtypeStruct((db, h, rows, d), F32)),
        grid=(db,),
        in_specs=[seq4(d), seq4(d), seq4(d), seq4(n_keys)],
        out_specs=(seq4(MOBA_TOPK * MOBA_BLOCK), seq4(LANES), seq4(d)),
        compiler_params=_params("parallel"),
        name="moba_sample_select",
    )(qp, knp, vnp, scores)
    sel = sel[:, :, :t, :MOBA_TOPK].reshape(db, h * t * MOBA_TOPK)

    seq2 = lambda last: pl.BlockSpec((None, h, rows, last), lambda bi, pt, sl: (bi, 0, 0, 0))
    out = pl.pallas_call(
        functools.partial(_sample_pv_kernel, n_new=t),
        out_shape=jax.ShapeDtypeStruct((db, h, rows, d), F32),
        grid_spec=pltpu.PrefetchScalarGridSpec(
            num_scalar_prefetch=2, grid=(db,),
            in_specs=[seq2(MOBA_TOPK * MOBA_BLOCK), seq2(d), pl.BlockSpec(memory_space=pl.ANY)],
            out_specs=seq2(d),
            scratch_shapes=[pltpu.VMEM((2, t * MOBA_TOPK, MOBA_BLOCK, d), F32),
                            pltpu.SemaphoreType.DMA((2,))]),
        compiler_params=_params("arbitrary"),
        name="moba_sample_pv",
    )(page_table, sel, probs, o_new, cache_v)
    return jnp.transpose(out[:, :, :t], (0, 2, 1, 3)).reshape(db, t, h * d).astype(BF16)


def _proj_ffn_kernel(a_ref, x_ref, wp_ref, g_ref, wg_ref, wu_ref, wd_ref, o_ref, *, chunks):
    hid = x_ref[...] + jnp.dot(a_ref[...], wp_ref[...], preferred_element_type=F32)
    n = _rms(hid, g_ref[...]).astype(BF16)
    width = wg_ref.shape[1] // chunks
    acc = hid
    for c in range(chunks):
        cols = slice(c * width, (c + 1) * width)
        gate = jnp.dot(n, wg_ref[:, cols], preferred_element_type=F32)
        up = jnp.dot(n, wu_ref[:, cols], preferred_element_type=F32)
        act = (_silu(gate) * up).astype(BF16)
        acc = acc + jnp.dot(act, wd_ref[cols, :], preferred_element_type=F32)
    o_ref[...] = acc


def _proj_ffn(a_bf, x, wp_bf, g, wg_bf, wu_bf, wd_bf, tm, chunks=2):
    m, d = x.shape
    ka = a_bf.shape[1]
    const = lambda shape: pl.BlockSpec(shape, lambda i: (0, 0), pipeline_mode=pl.Buffered(1))
    return pl.pallas_call(
        functools.partial(_proj_ffn_kernel, chunks=chunks),
        out_shape=jax.ShapeDtypeStruct((m, d), F32),
        grid=(m // tm,),
        in_specs=[pl.BlockSpec((tm, ka), lambda i: (i, 0)), pl.BlockSpec((tm, d), lambda i: (i, 0)),
                  const(wp_bf.shape), const((1, d)), const(wg_bf.shape), const(wu_bf.shape), const(wd_bf.shape)],
        out_specs=pl.BlockSpec((tm, d), lambda i: (i, 0)),
        compiler_params=_params("parallel"),
        name="proj_ffn",
    )(a_bf, x, wp_bf, g.reshape(1, d), wg_bf, wu_bf, wd_bf)


def _inproj_kernel(x_ref, g_ref, wz_ref, wx_ref, wdt_ref, z_ref, xbc_ref, dt_ref):
    n = _rms(x_ref[...], g_ref[...]).astype(BF16)
    z_ref[...] = jnp.dot(n, wz_ref[...], preferred_element_type=F32)
    xbc_ref[...] = jnp.dot(n, wx_ref[...], preferred_element_type=F32)
    dt_ref[...] = jnp.dot(n, wdt_ref[...], preferred_element_type=F32)


def _inproj(x, g, wz_bf, wx_bf, wdt_bf, tm):
    m, d = x.shape
    const = lambda shape: pl.BlockSpec(shape, lambda i: (0, 0), pipeline_mode=pl.Buffered(1))
    tile = lambda n: pl.BlockSpec((tm, n), lambda i: (i, 0))
    return pl.pallas_call(
        _inproj_kernel,
        out_shape=(jax.ShapeDtypeStruct((m, SSM_INNER), F32), jax.ShapeDtypeStruct((m, CONV_DIM), F32),
                   jax.ShapeDtypeStruct((m, LANES), F32)),
        grid=(m // tm,),
        in_specs=[tile(d), const((1, d)), const(wz_bf.shape), const(wx_bf.shape), const(wdt_bf.shape)],
        out_specs=(tile(SSM_INNER), tile(CONV_DIM), tile(LANES)),
        compiler_params=_params("parallel"),
        name="ssm_in_proj",
    )(x, g.reshape(1, d), wz_bf, wx_bf, wdt_bf)


def _ssd_kernel(*refs, rows, has_init):
    if has_init:
        (z_ref, xbc_ref, dt_ref, cw_ref, cb_ref, dtb_ref, alog_ref, dsk_ref, gn_ref, conv0_ref, ssm0_ref,
         y_ref, hout_ref, buf_ref, act_ref, ysc_ref, ht_ref) = refs
    else:
        (z_ref, xbc_ref, dt_ref, cw_ref, cb_ref, dtb_ref, alog_ref, dsk_ref, gn_ref,
         y_ref, hout_ref, buf_ref, act_ref, ysc_ref, ht_ref) = refs
    L = SSD_CHUNK
    P, N, G = SSM_HEADDIM, SSM_STATE, SSM_GROUPS
    PAIRS = SSM_HEADS // 2
    PPG = PAIRS // G
    GW = SSM_HPG * P
    assert L == N == 2 * P == LANES
    c = pl.program_id(1)
    last = pl.num_programs(1) - 1
    tail = SUBLANES - (SSM_CONV - 1)

    @pl.when(c == 0)
    def _():
        buf_ref[...] = jnp.zeros(buf_ref.shape, F32)
        if has_init:
            buf_ref[0:SUBLANES, :] = conv0_ref[...]
            for pair in range(PAIRS):
                two = jnp.concatenate([ssm0_ref[2 * pair], ssm0_ref[2 * pair + 1]], axis=0)
                ht_ref[pair] = two.T
        else:
            ht_ref[...] = jnp.zeros(ht_ref.shape, F32)

    buf_ref[SUBLANES:SUBLANES + rows, :] = xbc_ref[...]
    conv = cb_ref[...] + buf_ref[tail:tail + L, :] * cw_ref[0:1, :]
    for w in range(1, SSM_CONV):
        conv = conv + buf_ref[tail + w:tail + w + L, :] * cw_ref[w:w + 1, :]
    act_ref[...] = _silu(conv)
    if rows == L:
        buf_ref[0:SUBLANES, :] = buf_ref[L:L + SUBLANES, :]

    x = dt_ref[...] + dtb_ref[...]
    dt = jnp.maximum(x, 0.0) + jnp.log1p(jnp.exp(-jnp.abs(x)))
    if rows < L:
        dt = jnp.where(lax.broadcasted_iota(jnp.int32, dt.shape, 0) < rows, dt, 0.0)
    a = dt * (-jnp.exp(alog_ref[...]))
    ti = lax.broadcasted_iota(jnp.int32, (L, L), 0)
    si = lax.broadcasted_iota(jnp.int32, (L, L), 1)
    causal = ti >= si
    a_cs = jnp.dot(jnp.where(causal, 1.0, 0.0), a, precision=lax.Precision.HIGHEST,
                   preferred_element_type=F32)
    a_tot = a_cs[L - 1:L, :]
    ea_tot = jnp.exp(a_tot)
    a_cs_t = a_cs.T
    dt_t = dt.T
    dtde_t = (dt * jnp.exp(a_tot - a_cs)).T
    first = lax.broadcasted_iota(jnp.int32, (L, LANES), 1) < P

    for g in range(G):
        bg = act_ref[:, SSM_INNER + g * N:SSM_INNER + (g + 1) * N]
        cg = act_ref[:, SSM_INNER + G * N + g * N:SSM_INNER + G * N + (g + 1) * N]
        cb = lax.dot_general(cg.astype(BF16), bg.astype(BF16), _NT, preferred_element_type=F32)
        bg_t = bg.T
        for pr in range(PPG):
            pair = g * PPG + pr
            lanes = slice(pair * LANES, (pair + 1) * LANES)
            xs = act_ref[:, lanes]
            ht = ht_ref[pair]
            y = dsk_ref[:, lanes] * xs
            upd = jnp.zeros((N, LANES), F32)
            for second in range(2):
                e = 2 * pair + second
                mine = lambda v: (jnp.where(first, 0.0, v) if second else jnp.where(first, v, 0.0)).astype(BF16)
                xs_e, ht_e = mine(xs), mine(ht)
                acs = jnp.broadcast_to(a_cs[:, e:e + 1], (L, L))
                decay = jnp.exp(jnp.where(causal, acs - a_cs_t[e:e + 1, :], -jnp.inf))
                w = (cb * decay * dt_t[e:e + 1, :]).astype(BF16)
                c_in = (cg * jnp.exp(acs)).astype(BF16)
                y = (y + jnp.dot(w, xs_e, preferred_element_type=F32)
                     + jnp.dot(c_in, ht_e, preferred_element_type=F32))
                b_out = (bg_t * dtde_t[e:e + 1, :]).astype(BF16)
                upd = upd + jnp.dot(b_out, xs_e, preferred_element_type=F32)
            ysc_ref[:, lanes] = y
            keep = jnp.where(first[0:1, :], ea_tot[:, 2 * pair:2 * pair + 1], ea_tot[:, 2 * pair + 1:2 * pair + 2])
            ht_ref[pair] = keep * ht + upd

    z_in = z_ref[...]
    for g in range(G):
        lanes = slice(g * GW, (g + 1) * GW)
        yg = ysc_ref[0:rows, lanes] * _silu(z_in[:, lanes])
        y_ref[:, lanes] = _rms(yg, gn_ref[:, lanes]).astype(y_ref.dtype)

    @pl.when(c == last)
    def _():
        for pair in range(PAIRS):
            two = ht_ref[pair].T
            hout_ref[2 * pair] = two[0:P]
            hout_ref[2 * pair + 1] = two[P:2 * P]


def _ssd(z, xbc, dt, conv_w, conv_b, dt_bias, a_log, d_skip, gate_norm, conv0=None, ssm0=None):
    b, t, _ = z.shape
    has_init = conv0 is not None
    rows = min(SSD_CHUNK, t)
    assert t % rows == 0 and (rows == SSD_CHUNK or t == rows)
    nc = t // rows
    if rows < SSD_CHUNK:
        dt = jnp.pad(dt, ((0, 0), (0, SSD_CHUNK - rows), (0, 0)))
    lane_pad = lambda v: jnp.pad(v.astype(F32), (0, LANES - v.shape[0])).reshape(1, LANES)
    tile = lambda n: pl.BlockSpec((None, rows, n), lambda bi, ci: (bi, ci, 0))
    const = lambda shape: pl.BlockSpec(shape, lambda bi, ci: (0,) * len(shape))
    state_spec = pl.BlockSpec((None, SSM_HEADS, SSM_HEADDIM, SSM_STATE), lambda bi, ci: (bi, 0, 0, 0))
    in_specs = [tile(SSM_INNER), tile(CONV_DIM), pl.BlockSpec((None, SSD_CHUNK, LANES), lambda bi, ci: (bi, ci, 0)),
                const((SSM_CONV, CONV_DIM)), const((1, CONV_DIM)), const((1, LANES)), const((1, LANES)),
                const((1, SSM_INNER)), const((1, SSM_INNER))]
    args = [z, xbc, dt, conv_w, conv_b.reshape(1, CONV_DIM), lane_pad(dt_bias), lane_pad(a_log),
            jnp.repeat(d_skip.astype(F32), SSM_HEADDIM).reshape(1, SSM_INNER), gate_norm.reshape(1, SSM_INNER)]
    if has_init:
        in_specs += [pl.BlockSpec((None, SUBLANES, CONV_DIM), lambda bi, ci: (bi, 0, 0)), state_spec]
        args += [jnp.pad(conv0, ((0, 0), (SUBLANES - (SSM_CONV - 1), 0), (0, 0))), ssm0]
    y_dtype = BF16 if rows % 16 == 0 else F32
    return pl.pallas_call(
        functools.partial(_ssd_kernel, rows=rows, has_init=has_init),
        out_shape=(jax.ShapeDtypeStruct((b, t, SSM_INNER), y_dtype),
                   jax.ShapeDtypeStruct((b, SSM_HEADS, SSM_HEADDIM, SSM_STATE), F32)),
        grid=(b, nc),
        in_specs=in_specs,
        out_specs=(tile(SSM_INNER), state_spec),
        scratch_shapes=[pltpu.VMEM((SUBLANES + SSD_CHUNK, CONV_DIM), F32),
                        pltpu.VMEM((SSD_CHUNK, CONV_DIM), F32),
                        pltpu.VMEM((SSD_CHUNK, SSM_INNER), F32),
                        pltpu.VMEM((SSM_HEADS // 2, SSM_STATE, 2 * SSM_HEADDIM), F32)],
        compiler_params=_params("parallel", "arbitrary"),
        name="ssd_chunk",
    )(*args)


def kernel(x_prompt, x_sample, cache_k, cache_v, page_table, state_conv, state_ssm, attn_norm, w_qkv, q_norm, k_norm, w_o, ssm_norm, w_in, conv_w, conv_b, dt_bias, a_log, d_skip, gate_norm, w_out, ffn_norm, w_gate, w_up, w_down):
    b, s, d = x_prompt.shape
    db, t, _ = x_sample.shape
    past = page_table.shape[1] * PAGE_SIZE
    tm_p, tm_s = 512, db * t
    bf = lambda w: w.astype(BF16)

    w_qkv_bf, w_o_bf = bf(w_qkv[0]), bf(w_o[0])
    ffn0 = (ffn_norm[0], bf(w_gate[0]), bf(w_up[0]), bf(w_down[0]))
    q_p, k_p, v_p = _qkv(x_prompt, attn_norm[0], w_qkv_bf, q_norm[0], k_norm[0], jnp.arange(s), tm_p)
    attn_p = _moba_prompt(q_p, k_p, v_p)
    pos_s = past + jnp.tile(jnp.arange(t), db)
    xs_rows = x_sample.reshape(1, db * t, d)
    qkv_s = _qkv(xs_rows, attn_norm[0], w_qkv_bf, q_norm[0], k_norm[0], pos_s, tm_s)
    q_s, k_s, v_s = (jnp.transpose(a.reshape(N_HEADS, db, t, HEAD_DIM), (1, 0, 2, 3)) for a in qkv_s)
    attn_s = _moba_sample(q_s, k_s, v_s, cache_k[0], cache_v[0], page_table)
    h_p = _proj_ffn(attn_p.reshape(b * s, d), x_prompt.reshape(b * s, d), w_o_bf, *ffn0, tm_p)
    h_s = _proj_ffn(attn_s.reshape(db * t, d), x_sample.reshape(db * t, d), w_o_bf, *ffn0, tm_s)

    w_in1 = w_in[0]
    wz_bf = bf(w_in1[:, :SSM_INNER])
    wx_bf = bf(w_in1[:, SSM_INNER:SSM_INNER + CONV_DIM])
    wdt_bf = bf(jnp.pad(w_in1[:, SSM_INNER + CONV_DIM:], ((0, 0), (0, LANES - SSM_HEADS))))
    w_out_bf = bf(w_out[0])
    ffn1 = (ffn_norm[1], bf(w_gate[1]), bf(w_up[1]), bf(w_down[1]))
    ssd_w = (conv_w[0], conv_b[0], dt_bias[0], a_log[0], d_skip[0], gate_norm[0])
    z_p, xbc_p, dt_p = _inproj(h_p, ssm_norm[0], wz_bf, wx_bf, wdt_bf, 256)
    z_s, xbc_s, dt_s = _inproj(h_s, ssm_norm[0], wz_bf, wx_bf, wdt_bf, tm_s)
    xbc_p = xbc_p.reshape(b, s, CONV_DIM)
    xbc_s = xbc_s.reshape(db, t, CONV_DIM)
    y_p, ssm_p = _ssd(z_p.reshape(b, s, SSM_INNER), xbc_p, dt_p.reshape(b, s, LANES), *ssd_w)
    y_s, ssm_s = _ssd(z_s.reshape(db, t, SSM_INNER), xbc_s, dt_s.reshape(db, t, LANES), *ssd_w,
                      conv0=state_conv[0], ssm0=state_ssm[0])
    out_p = _proj_ffn(y_p.reshape(b * s, SSM_INNER), h_p, w_out_bf, *ffn1, tm_p)
    out_s = _proj_ffn(bf(y_s).reshape(db * t, SSM_INNER), h_s, w_out_bf, *ffn1, tm_s)

    keep = SSM_CONV - 1
    conv_p = xbc_p[:, s - keep:]
    conv_s = jnp.concatenate([state_conv[0], xbc_s], axis=1)[:, t:]
    return (out_p.reshape(b, s, d), out_s.reshape(db, t, d), k_p[None], v_p[None], k_s[None], v_s[None],
            conv_p[None], ssm_p[None], conv_s[None], ssm_s[None])
```

```python
import functools

import jax
import jax.numpy as jnp
from jax import lax
from jax.experimental import pallas as pl
from jax.experimental.pallas import tpu as pltpu

F32 = jnp.float32
BF16 = jnp.bfloat16

EPS = 1e-6
N_HEADS = 8
HEAD_DIM = 128
ROT_DIM = 32
ROPE_THETA = 500000.0
MOBA_BLOCK = 256
MOBA_TOPK = 3
PAGE_SIZE = 128
SSM_INNER = 2048
SSM_HEADDIM = 64
SSM_HEADS = 32
SSM_GROUPS = 4
SSM_HPG = SSM_HEADS // SSM_GROUPS
SSM_STATE = 128
SSM_CONV = 4
CONV_DIM = SSM_INNER + 2 * SSM_GROUPS * SSM_STATE
SSD_CHUNK = 128

LANES = 128
SUBLANES = 8
VMEM_LIMIT_BYTES = 56 * 2**20
MOBA_UNROLL = 4
MOBA_HEADS_PER_STEP = 4
PAGES_PER_STEP = 16
INPROJ_SLAB = 512
SAMPLE_Q_ROWS = 8

LOG2_E = 1.4426950408889634

_NT = (((1,), (1,)), ((), ()))


def _params(*semantics):
    return pltpu.CompilerParams(dimension_semantics=semantics, vmem_limit_bytes=VMEM_LIMIT_BYTES)


def _rms(x, g):
    return x * lax.rsqrt(jnp.mean(x * x, axis=-1, keepdims=True) + EPS) * g


def _silu(x):
    return x / (1.0 + jnp.exp(-x))


def _qkv_kernel(x_ref, g_ref, w_ref, qg_ref, kg_ref, cos_ref, sa_ref, sb_ref, q_ref, k_ref, v_ref):
    n = _rms(x_ref[...], g_ref[...]).astype(BF16)
    cos, sa, sb = cos_ref[...], sa_ref[...], sb_ref[...]
    d = N_HEADS * HEAD_DIM
    for part, (o_ref, gain_ref) in enumerate(((q_ref, qg_ref), (k_ref, kg_ref), (v_ref, None))):
        y = jnp.dot(n, w_ref[:, part * d:(part + 1) * d], preferred_element_type=F32)
        for h in range(N_HEADS):
            yh = y[:, h * HEAD_DIM:(h + 1) * HEAD_DIM]
            if gain_ref is not None:
                yh = _rms(yh, gain_ref[...])
                yh = (yh * cos + pltpu.roll(yh, HEAD_DIM - ROT_DIM // 2, 1) * sa
                      + pltpu.roll(yh, ROT_DIM // 2, 1) * sb)
            o_ref[h] = yh


def _rope_tables(pos):
    half = ROT_DIM // 2
    inv = ROPE_THETA ** (-jnp.arange(half, dtype=F32) * 2.0 / ROT_DIM)
    ang = pos.astype(F32)[:, None] * inv[None, :]
    cos, sin = jnp.cos(ang), jnp.sin(ang)
    t = pos.shape[0]
    cos_t = jnp.concatenate([cos, cos, jnp.ones((t, HEAD_DIM - ROT_DIM), F32)], axis=1)
    sa = jnp.concatenate([-sin, jnp.zeros((t, HEAD_DIM - half), F32)], axis=1)
    sb = jnp.concatenate([jnp.zeros((t, half), F32), sin, jnp.zeros((t, HEAD_DIM - ROT_DIM), F32)], axis=1)
    return cos_t, sa, sb


def _qkv(x, g, w_bf, qg, kg, pos, tm):
    b, t, d = x.shape
    cos_t, sa, sb = _rope_tables(pos)
    row = lambda v: v.reshape(1, -1)
    out = jax.ShapeDtypeStruct((b, N_HEADS, t, HEAD_DIM), F32)
    tab_spec = pl.BlockSpec((tm, HEAD_DIM), lambda bi, i: (i, 0))
    out_spec = pl.BlockSpec((None, N_HEADS, tm, HEAD_DIM), lambda bi, i: (bi, 0, i, 0))
    const = lambda shape: pl.BlockSpec(shape, lambda bi, i: (0,) * len(shape))
    return pl.pallas_call(
        _qkv_kernel,
        out_shape=(out, out, out),
        grid=(b, t // tm),
        in_specs=[pl.BlockSpec((None, tm, d), lambda bi, i: (bi, i, 0)),
                  const((1, d)), const(w_bf.shape), const((1, HEAD_DIM)), const((1, HEAD_DIM)),
                  tab_spec, tab_spec, tab_spec],
        out_specs=(out_spec, out_spec, out_spec),
        compiler_params=_params("parallel", "parallel"),
        name="qkv_norm_rope",
    )(x, row(g), w_bf, row(qg), row(kg), cos_t, sa, sb)


def _topk_rank(entries, gate, index):
    rank = jnp.zeros(gate.shape, F32)
    for j, c in enumerate(entries):
        ahead = (c > gate) | ((c == gate) & (j < index))
        rank = rank + jnp.where(ahead, 1.0, 0.0)
    return rank


def _moba_prompt_kernel(q_ref, k_ref, v_ref, o_ref, kb_ref, vt_ref, km_ref, bias_ref):
    i = pl.program_id(2)
    heads = q_ref.shape[0]
    nb = k_ref.shape[1] // MOBA_BLOCK
    scale = HEAD_DIM ** -0.5

    @pl.when(i == 0)
    def _():
        kb_ref[...] = k_ref[...].astype(BF16)
        km_ref[...] = jnp.zeros(km_ref.shape, F32)
        for h in range(heads):
            for j in range(nb):
                rows = slice(j * MOBA_BLOCK, (j + 1) * MOBA_BLOCK)
                vt_ref[h, j] = v_ref[h, rows, :].T.astype(BF16)
                km_ref[h, j:j + 1, :] = jnp.mean(k_ref[h, rows, :], axis=0, keepdims=True)

    qbs = []
    for h in range(heads):
        qf = q_ref[h]
        qbs.append((qf * (scale * LOG2_E)).astype(BF16))
        gate = lax.dot_general(km_ref[h], qf, _NT, precision=lax.Precision.HIGHEST,
                               preferred_element_type=F32)
        blk = lax.broadcasted_iota(jnp.int32, gate.shape, 0)
        past = blk < i
        gate = jnp.where(past, gate, -jnp.inf)
        rank = _topk_rank([gate[j:j + 1, :] for j in range(nb)], gate, blk)
        bias_ref[h] = jnp.where((rank < MOBA_TOPK) & past, 0.0, -jnp.inf)

    def scores(h, j):
        start = j * MOBA_BLOCK
        if not isinstance(j, int):
            start = pl.multiple_of(start, MOBA_BLOCK)
        kj = kb_ref[h, pl.ds(start, MOBA_BLOCK), :]
        return lax.dot_general(kj, qbs[h], _NT, preferred_element_type=F32)

    def attend(n_lead):
        for h in range(heads):
            s = scores(h, i)
            ki = lax.broadcasted_iota(jnp.int32, s.shape, 0)
            qi = lax.broadcasted_iota(jnp.int32, s.shape, 1)
            s_own = jnp.where(ki <= qi, s, -jnp.inf)
            lead = [(j, scores(h, j), bias_ref[h, j:j + 1, :]) for j in range(n_lead)]
            m = jnp.max(s_own, axis=0, keepdims=True)
            for _, s, bias in lead:
                m = jnp.maximum(m, jnp.max(s, axis=0, keepdims=True) + bias)
            blocks = [(i, s_own, m)] + [(j, s, m - bias) for j, s, bias in lead]
            l, acc = 0.0, 0.0
            for j, s, shift in blocks:
                p = jnp.exp2(s - shift)
                l = l + jnp.sum(p, axis=0, keepdims=True)
                acc = acc + jnp.dot(vt_ref[h, j], p.astype(BF16), preferred_element_type=F32)
            o_ref[h * HEAD_DIM:(h + 1) * HEAD_DIM, :] = (acc / l).astype(o_ref.dtype)

    groups = pl.cdiv(i, MOBA_UNROLL)
    for v in range(nb // MOBA_UNROLL + 1):
        pl.when(groups == v)(functools.partial(attend, v * MOBA_UNROLL))


def _moba_prompt(q, k, v):
    b, h, s, d = q.shape
    hs = MOBA_HEADS_PER_STEP
    assert s % (MOBA_BLOCK * MOBA_UNROLL) == 0 and h % hs == 0
    tq = MOBA_BLOCK
    nb = s // MOBA_BLOCK
    nb_pad = pl.cdiv(nb, SUBLANES) * SUBLANES
    full = pl.BlockSpec((None, hs, s, d), lambda bi, hi, i: (bi, hi, 0, 0))
    return pl.pallas_call(
        _moba_prompt_kernel,
        out_shape=jax.ShapeDtypeStruct((b, h * d, s), BF16),
        grid=(b, h // hs, s // tq),
        in_specs=[pl.BlockSpec((None, hs, tq, d), lambda bi, hi, i: (bi, hi, i, 0)), full, full],
        out_specs=pl.BlockSpec((None, hs * d, tq), lambda bi, hi, i: (bi, hi, i)),
        scratch_shapes=[pltpu.VMEM((hs, s, d), BF16),
                        pltpu.VMEM((hs, nb, d, MOBA_BLOCK), BF16),
                        pltpu.VMEM((hs, nb_pad, d), F32),
                        pltpu.VMEM((hs, nb_pad, tq), F32)],
        compiler_params=_params("parallel", "parallel", "arbitrary"),
        name="moba_prompt",
    )(q, k, v)


def _sample_scores_kernel(pt_ref, q_ref, *refs):
    del pt_ref
    k_refs, s_ref = refs[:-1], refs[-1]
    for r, k_ref in enumerate(k_refs):
        for h in range(N_HEADS):
            s = lax.dot_general(q_ref[h].astype(BF16), k_ref[h].astype(BF16), _NT, preferred_element_type=F32)
            s_ref[h, :, r * PAGE_SIZE:(r + 1) * PAGE_SIZE] = s


def _sample_select_kernel(q_ref, kn_ref, vn_ref, s_ref, pc_ref, sel_ref, ol_ref, *, n_new, n_full):
    scale = HEAD_DIM ** -0.5
    rows = s_ref.shape[1]
    n_top = min(MOBA_TOPK, n_full)
    lane = lax.broadcasted_iota(jnp.int32, (rows, LANES), 1)
    lane_f = lane.astype(F32)
    row = lax.broadcasted_iota(jnp.int32, (rows, 1), 0)
    for h in range(N_HEADS):
        s = s_ref[h]
        blocks = [s[:, j * MOBA_BLOCK:(j + 1) * MOBA_BLOCK] for j in range(n_full)]
        cols = [jnp.sum(blk, axis=1, keepdims=True) * (1.0 / MOBA_BLOCK) for blk in blocks]
        gate = jnp.full((rows, LANES), -jnp.inf, F32)
        for j, c in enumerate(cols):
            gate = jnp.where(lane == j, c, gate)
        rank = _topk_rank(cols, gate, lane)
        picks = [jnp.sum(jnp.where((rank == r) & (lane < n_full), lane_f, 0.0), axis=1, keepdims=True)
                 for r in range(n_top)]
        s_sel = []
        for pick in picks:
            g = jnp.zeros((rows, MOBA_BLOCK), F32)
            for j, blk in enumerate(blocks):
                g = g + jnp.where(pick == j, blk, 0.0)
            s_sel.append(g * scale)
        q = q_ref[h]
        s_new = []
        for u in range(n_new):
            su = jnp.sum(q * kn_ref[h, u:u + 1, :], axis=1, keepdims=True) * scale
            s_new.append(jnp.where(row >= u, su, -jnp.inf))
        m = functools.reduce(jnp.maximum, [jnp.max(g, axis=1, keepdims=True) for g in s_sel] + s_new)
        p_sel = [jnp.exp(g - m) for g in s_sel]
        p_new = [jnp.exp(su - m) for su in s_new]
        denom = functools.reduce(jnp.add, [jnp.sum(p, axis=1, keepdims=True) for p in p_sel] + p_new)
        ids = jnp.zeros((rows, LANES), F32)
        for r in range(MOBA_TOPK):
            cols_r = slice(r * MOBA_BLOCK, (r + 1) * MOBA_BLOCK)
            if r < n_top:
                pc_ref[h, :, cols_r] = p_sel[r] / denom
                ids = jnp.where(lane == r, picks[r], ids)
            else:
                pc_ref[h, :, cols_r] = jnp.zeros((rows, MOBA_BLOCK), F32)
        sel_ref[h] = ids.astype(jnp.int32)
        o = jnp.zeros((rows, HEAD_DIM), F32)
        for u, pu in enumerate(p_new):
            o = o + (pu / denom) * vn_ref[h, u:u + 1, :]
        ol_ref[h] = o


def _sample_pv_kernel(pt_ref, sel_ref, pc_ref, ol_ref, v_hbm, o_ref, vbuf, sems, *, n_new):
    b = pl.program_id(0)
    n_seq = pl.num_programs(0)
    rows = pc_ref.shape[1]
    ppb = MOBA_BLOCK // PAGE_SIZE
    row = lax.broadcasted_iota(jnp.int32, (rows, MOBA_BLOCK), 0)

    def copies(seq, h):
        out = []
        for t in range(n_new):
            for r in range(MOBA_TOPK):
                k = t * MOBA_TOPK + r
                blk = sel_ref[seq, (h * n_new + t) * MOBA_TOPK + r]
                for half in range(ppb):
                    page = pt_ref[seq, blk * ppb + half]
                    out.append(pltpu.make_async_copy(
                        v_hbm.at[page, h], vbuf.at[h % 2, k, pl.ds(half * PAGE_SIZE, PAGE_SIZE), :], sems.at[h % 2]))
        return out

    @pl.when(b == 0)
    def _():
        for c in copies(b, 0):
            c.start()

    for h in range(N_HEADS):
        if h + 1 < N_HEADS:
            for c in copies(b, h + 1):
                c.start()
        else:
            @pl.when(b + 1 < n_seq)
            def _():
                for c in copies(b + 1, 0):
                    c.start()
        for c in copies(b, h):
            c.wait()
        acc = ol_ref[h]
        for t in range(n_new):
            for r in range(MOBA_TOPK):
                p = jnp.where(row == t, pc_ref[h, :, r * MOBA_BLOCK:(r + 1) * MOBA_BLOCK], 0.0).astype(BF16)
                acc = acc + jnp.dot(p, vbuf[h % 2, t * MOBA_TOPK + r].astype(BF16), preferred_element_type=F32)
        o_ref[h] = acc


def _page_specs(n):
    def spec(r):
        return pl.BlockSpec((None, N_HEADS, PAGE_SIZE, HEAD_DIM),
                            lambda bi, g, pt: (pt[bi, g * n + r], 0, 0, 0))
    return [spec(r) for r in range(n)]


def _moba_sample(q, k_new, v_new, cache_k, cache_v, page_table):
    db, h, t, d = q.shape
    n_pages = page_table.shape[1]
    assert (n_pages * PAGE_SIZE) % MOBA_BLOCK == 0, "cached keys must end on a MoBA block boundary"
    assert n_pages % PAGES_PER_STEP == 0 and t <= SAMPLE_Q_ROWS and h % 2 == 0
    n_full = n_pages * PAGE_SIZE // MOBA_BLOCK
    n_keys = n_pages * PAGE_SIZE
    rows = SAMPLE_Q_ROWS
    steps = n_pages // PAGES_PER_STEP
    pad = lambda a: jnp.pad(a, ((0, 0), (0, 0), (0, rows - t), (0, 0)))
    qp, knp, vnp = pad(q), pad(k_new), pad(v_new)
    per_seq = pl.BlockSpec((None, h, rows, d), lambda bi, g, pt: (bi, 0, 0, 0))
    keys_spec = pl.BlockSpec((None, h, rows, PAGES_PER_STEP * PAGE_SIZE), lambda bi, g, pt: (bi, 0, 0, g))

    scores = pl.pallas_call(
        _sample_scores_kernel,
        out_shape=jax.ShapeDtypeStruct((db, h, rows, n_keys), F32),
        grid_spec=pltpu.PrefetchScalarGridSpec(
            num_scalar_prefetch=1, grid=(db, steps),
            in_specs=[per_seq] + _page_specs(PAGES_PER_STEP), out_specs=keys_spec),
        compiler_params=_params("parallel", "arbitrary"),
        name="moba_sample_scores",
    )(page_table, qp, *([cache_k] * PAGES_PER_STEP))

    seq4 = lambda last: pl.BlockSpec((None, h, rows, last), lambda bi: (bi, 0, 0, 0))
    probs, sel, o_new = pl.pallas_call(
        functools.partial(_sample_select_kernel, n_new=t, n_full=n_full),
        out_shape=(jax.ShapeDtypeStruct((db, h, rows, MOBA_TOPK * MOBA_BLOCK), F32),
                   jax.ShapeDtypeStruct((db, h, rows, LANES), jnp.int32),
                   jax.ShapeDtypeStruct((db, h, rows, d), F32)),
        grid=(db,),
        in_specs=[seq4(d), seq4(d), seq4(d), seq4(n_keys)],
        out_specs=(seq4(MOBA_TOPK * MOBA_BLOCK), seq4(LANES), seq4(d)),
        compiler_params=_params("parallel"),
        name="moba_sample_select",
    )(qp, knp, vnp, scores)
    sel = sel[:, :, :t, :MOBA_TOPK].reshape(db, h * t * MOBA_TOPK)

    seq2 = lambda last: pl.BlockSpec((None, h, rows, last), lambda bi, pt, sl: (bi, 0, 0, 0))
    out = pl.pallas_call(
        functools.partial(_sample_pv_kernel, n_new=t),
        out_shape=jax.ShapeDtypeStruct((db, h, rows, d), F32),
        grid_spec=pltpu.PrefetchScalarGridSpec(
            num_scalar_prefetch=2, grid=(db,),
            in_specs=[seq2(MOBA_TOPK * MOBA_BLOCK), seq2(d), pl.BlockSpec(memory_space=pl.ANY)],
            out_specs=seq2(d),
            scratch_shapes=[pltpu.VMEM((2, t * MOBA_TOPK, MOBA_BLOCK, d), F32),
                            pltpu.SemaphoreType.DMA((2,))]),
        compiler_params=_params("arbitrary"),
        name="moba_sample_pv",
    )(page_table, sel, probs, o_new, cache_v)
    return jnp.transpose(out[:, :, :t], (0, 2, 1, 3)).reshape(db, t, h * d).astype(BF16)


def _proj_ffn_kernel(a_ref, x_ref, wp_ref, g_ref, wg_ref, wu_ref, wd_ref, o_ref, *, chunks, a_transposed):
    a = a_ref[...].T if a_transposed else a_ref[...]
    hid = x_ref[...] + jnp.dot(a, wp_ref[...], preferred_element_type=F32)
    n = _rms(hid, g_ref[...]).astype(BF16)
    width = wg_ref.shape[1] // chunks
    acc = hid
    for c in range(chunks):
        cols = slice(c * width, (c + 1) * width)
        gate = jnp.dot(n, wg_ref[:, cols], preferred_element_type=F32)
        up = jnp.dot(n, wu_ref[:, cols], preferred_element_type=F32)
        act = (_silu(gate) * up).astype(BF16)
        acc = acc + jnp.dot(act, wd_ref[cols, :], preferred_element_type=F32)
    o_ref[...] = acc


def _proj_ffn(a_bf, x, wp_bf, g, wg_bf, wu_bf, wd_bf, layer, tm, chunks=2):
    m, d = x.shape
    stacked = lambda w: pl.BlockSpec((None,) + w.shape[1:], lambda i: (layer, 0, 0), pipeline_mode=pl.Buffered(1))
    a_transposed = a_bf.ndim == 3
    if a_transposed:
        nb, ka, per = a_bf.shape
        assert nb * per == m and per % tm == 0
        a_spec = pl.BlockSpec((None, ka, tm), lambda i: (i // (per // tm), 0, i % (per // tm)))
    else:
        ka = a_bf.shape[1]
        a_spec = pl.BlockSpec((tm, ka), lambda i: (i, 0))
    const = lambda shape: pl.BlockSpec(shape, lambda i: (0, 0), pipeline_mode=pl.Buffered(1))
    return pl.pallas_call(
        functools.partial(_proj_ffn_kernel, chunks=chunks, a_transposed=a_transposed),
        out_shape=jax.ShapeDtypeStruct((m, d), F32),
        grid=(m // tm,),
        in_specs=[a_spec, pl.BlockSpec((tm, d), lambda i: (i, 0)),
                  const(wp_bf.shape), const((1, d)), stacked(wg_bf), stacked(wu_bf), stacked(wd_bf)],
        out_specs=pl.BlockSpec((tm, d), lambda i: (i, 0)),
        compiler_params=_params("parallel"),
        name="proj_ffn",
    )(a_bf, x, wp_bf, g[layer].reshape(1, d), wg_bf, wu_bf, wd_bf)


def _inproj_kernel(x_ref, g_ref, wz_ref, wx_ref, wdt_ref, z_ref, xbc_ref, dt_ref):
    n = _rms(x_ref[...], g_ref[...]).astype(BF16)
    z_ref[...] = jnp.dot(n, wz_ref[...], preferred_element_type=F32)
    xbc_ref[...] = jnp.dot(n, wx_ref[...], preferred_element_type=F32)
    dt_ref[...] = jnp.dot(n, wdt_ref[...], preferred_element_type=F32)


def _inproj_conv_kernel(x_ref, g_ref, wz_ref, wx_ref, wdt_ref, cw_ref, cb_ref, zs_ref, act_ref, dt_ref, tail_ref,
                        buf_ref, *, tiles_per_seq):
    tm = x_ref.shape[0]

    @pl.when(pl.program_id(0) % tiles_per_seq == 0)
    def _():
        buf_ref[...] = jnp.zeros((SUBLANES, CONV_DIM), F32)

    n = _rms(x_ref[...], g_ref[...]).astype(BF16)
    groups = tm // SUBLANES
    sub = lax.broadcasted_iota(jnp.int32, (groups, SUBLANES, INPROJ_SLAB), 1)
    for c0 in range(0, CONV_DIM, INPROJ_SLAB):
        cols = slice(c0, c0 + INPROJ_SLAB)
        xs = jnp.dot(n, wx_ref[:, cols], preferred_element_type=F32)
        ext = jnp.concatenate([buf_ref[:, cols], xs], axis=0).reshape(groups + 1, SUBLANES, INPROJ_SLAB)
        conv = cb_ref[:, cols] + xs * cw_ref[SSM_CONV - 1:SSM_CONV, cols]
        for k in range(1, SSM_CONV):
            rot = pltpu.roll(ext, k, 1)
            shifted = jnp.where(sub < k, rot[:-1], rot[1:]).reshape(tm, INPROJ_SLAB)
            conv = conv + shifted * cw_ref[SSM_CONV - 1 - k:SSM_CONV - k, cols]
        act_ref[:, cols] = _silu(conv)
        newest = xs[tm - SUBLANES:tm]
        buf_ref[:, cols] = newest
        tail_ref[:, cols] = newest
    for c0 in range(0, SSM_INNER, INPROJ_SLAB):
        cols = slice(c0, c0 + INPROJ_SLAB)
        zs_ref[:, cols] = _silu(jnp.dot(n, wz_ref[:, cols], preferred_element_type=F32))
    dt_ref[...] = jnp.dot(n, wdt_ref[...], preferred_element_type=F32)


def _inproj_conv(x, g, wz_bf, wx_bf, wdt_bf, conv_w, conv_b, tm, seq_len):
    m, d = x.shape
    assert seq_len % tm == 0 and m % seq_len == 0
    tiles_per_seq = seq_len // tm
    const = lambda shape: pl.BlockSpec(shape, lambda i: (0, 0), pipeline_mode=pl.Buffered(1))
    tile = lambda n: pl.BlockSpec((tm, n), lambda i: (i, 0))
    return pl.pallas_call(
        functools.partial(_inproj_conv_kernel, tiles_per_seq=tiles_per_seq),
        out_shape=(jax.ShapeDtypeStruct((m, SSM_INNER), F32), jax.ShapeDtypeStruct((m, CONV_DIM), F32),
                   jax.ShapeDtypeStruct((m, LANES), F32),
                   jax.ShapeDtypeStruct((m // seq_len, SUBLANES, CONV_DIM), F32)),
        grid=(m // tm,),
        in_specs=[tile(d), const((1, d)), const(wz_bf.shape), const(wx_bf.shape), const(wdt_bf.shape),
                  const((SSM_CONV, CONV_DIM)), const((1, CONV_DIM))],
        out_specs=(tile(SSM_INNER), tile(CONV_DIM), tile(LANES),
                   pl.BlockSpec((None, SUBLANES, CONV_DIM), lambda i: (i // tiles_per_seq, 0, 0))),
        scratch_shapes=[pltpu.VMEM((SUBLANES, CONV_DIM), F32)],
        compiler_params=_params("arbitrary"),
        name="ssm_in_proj_conv",
    )(x, g.reshape(1, d), wz_bf, wx_bf, wdt_bf, conv_w, conv_b.reshape(1, CONV_DIM))


def _inproj(x, g, wz_bf, wx_bf, wdt_bf, tm):
    m, d = x.shape
    const = lambda shape: pl.BlockSpec(shape, lambda i: (0, 0), pipeline_mode=pl.Buffered(1))
    tile = lambda n: pl.BlockSpec((tm, n), lambda i: (i, 0))
    return pl.pallas_call(
        _inproj_kernel,
        out_shape=(jax.ShapeDtypeStruct((m, SSM_INNER), F32), jax.ShapeDtypeStruct((m, CONV_DIM), F32),
                   jax.ShapeDtypeStruct((m, LANES), F32)),
        grid=(m // tm,),
        in_specs=[tile(d), const((1, d)), const(wz_bf.shape), const(wx_bf.shape), const(wdt_bf.shape)],
        out_specs=(tile(SSM_INNER), tile(CONV_DIM), tile(LANES)),
        compiler_params=_params("parallel"),
        name="ssm_in_proj",
    )(x, g.reshape(1, d), wz_bf, wx_bf, wdt_bf)


def _ssd_kernel(*refs, rows, has_init, activated):
    it = iter(refs)
    z_ref, xbc_ref, dt_ref = next(it), next(it), next(it)
    cw_ref, cb_ref = (None, None) if activated else (next(it), next(it))
    dtb_ref, alog_ref, dsk_ref, gn_ref = next(it), next(it), next(it), next(it)
    conv0_ref, ssm0_ref = (next(it), next(it)) if has_init else (None, None)
    y_ref, hout_ref = next(it), next(it)
    buf_ref, act_ref = (None, xbc_ref) if activated else (next(it), next(it))
    ysc_ref, ht_ref = next(it), next(it)
    assert not (activated and (has_init or rows != SSD_CHUNK))
    L = SSD_CHUNK
    P, N, G = SSM_HEADDIM, SSM_STATE, SSM_GROUPS
    PAIRS = SSM_HEADS // 2
    PPG = PAIRS // G
    GW = SSM_HPG * P
    assert L == N == 2 * P == LANES
    c = pl.program_id(1)
    last = pl.num_programs(1) - 1
    tail = SUBLANES - (SSM_CONV - 1)

    @pl.when(c == 0)
    def _():
        if not activated:
            buf_ref[...] = jnp.zeros(buf_ref.shape, F32)
        if has_init:
            buf_ref[0:SUBLANES, :] = conv0_ref[...]
            for pair in range(PAIRS):
                two = jnp.concatenate([ssm0_ref[2 * pair], ssm0_ref[2 * pair + 1]], axis=0)
                ht_ref[pair] = two.T
        else:
            ht_ref[...] = jnp.zeros(ht_ref.shape, F32)

    if not activated:
        buf_ref[SUBLANES:SUBLANES + rows, :] = xbc_ref[...]
        conv = cb_ref[...] + buf_ref[tail:tail + L, :] * cw_ref[0:1, :]
        for w in range(1, SSM_CONV):
            conv = conv + buf_ref[tail + w:tail + w + L, :] * cw_ref[w:w + 1, :]
        act_ref[...] = _silu(conv)
        if rows == L:
            buf_ref[0:SUBLANES, :] = buf_ref[L:L + SUBLANES, :]

    x = dt_ref[...] + dtb_ref[...]
    dt = jnp.maximum(x, 0.0) + jnp.log1p(jnp.exp(-jnp.abs(x)))
    if rows < L:
        dt = jnp.where(lax.broadcasted_iota(jnp.int32, dt.shape, 0) < rows, dt, 0.0)
    a = dt * (-jnp.exp(alog_ref[...]))
    ti = lax.broadcasted_iota(jnp.int32, (L, L), 0)
    si = lax.broadcasted_iota(jnp.int32, (L, L), 1)
    causal = ti >= si
    a_cs = jnp.dot(jnp.where(causal, 1.0, 0.0), a, precision=lax.Precision.HIGHEST,
                   preferred_element_type=F32)
    a_tot = a_cs[L - 1:L, :]
    ea_tot = jnp.exp(a_tot)
    a_cs_t = a_cs.T
    dt_t = dt.T
    dtde_t = (dt * jnp.exp(a_tot - a_cs)).T
    first = lax.broadcasted_iota(jnp.int32, (L, LANES), 1) < P

    for g in range(G):
        bg = act_ref[:, SSM_INNER + g * N:SSM_INNER + (g + 1) * N]
        cg = act_ref[:, SSM_INNER + G * N + g * N:SSM_INNER + G * N + (g + 1) * N]
        cb = lax.dot_general(cg.astype(BF16), bg.astype(BF16), _NT, preferred_element_type=F32)
        bg_t = bg.T
        for pr in range(PPG):
            pair = g * PPG + pr
            lanes = slice(pair * LANES, (pair + 1) * LANES)
            xs = act_ref[:, lanes]
            ht = ht_ref[pair]
            y = dsk_ref[:, lanes] * xs
            upd = jnp.zeros((N, LANES), F32)
            for second in range(2):
                e = 2 * pair + second
                mine = lambda v: (jnp.where(first, 0.0, v) if second else jnp.where(first, v, 0.0)).astype(BF16)
                xs_e, ht_e = mine(xs), mine(ht)
                acs = jnp.broadcast_to(a_cs[:, e:e + 1], (L, L))
                decay = jnp.exp(jnp.where(causal, acs - a_cs_t[e:e + 1, :], -jnp.inf))
                w = (cb * decay * dt_t[e:e + 1, :]).astype(BF16)
                c_in = (cg * jnp.exp(acs)).astype(BF16)
                y = (y + jnp.dot(w, xs_e, preferred_element_type=F32)
                     + jnp.dot(c_in, ht_e, preferred_element_type=F32))
                b_out = (bg_t * dtde_t[e:e + 1, :]).astype(BF16)
                upd = upd + jnp.dot(b_out, xs_e, preferred_element_type=F32)
            ysc_ref[:, lanes] = y
            keep = jnp.where(first[0:1, :], ea_tot[:, 2 * pair:2 * pair + 1], ea_tot[:, 2 * pair + 1:2 * pair + 2])
            ht_ref[pair] = keep * ht + upd

    z_in = z_ref[...]
    for g in range(G):
        lanes = slice(g * GW, (g + 1) * GW)
        yg = ysc_ref[0:rows, lanes] * (z_in[:, lanes] if activated else _silu(z_in[:, lanes]))
        y_ref[:, lanes] = _rms(yg, gn_ref[:, lanes]).astype(y_ref.dtype)

    @pl.when(c == last)
    def _():
        for pair in range(PAIRS):
            two = ht_ref[pair].T
            hout_ref[2 * pair] = two[0:P]
            hout_ref[2 * pair + 1] = two[P:2 * P]


def _ssd(z, xbc, dt, dt_bias, a_log, d_skip, gate_norm, conv=None, conv0=None, ssm0=None):
    b, t, _ = z.shape
    has_init = conv0 is not None
    activated = conv is None
    rows = min(SSD_CHUNK, t)
    assert t % rows == 0 and (rows == SSD_CHUNK or t == rows)
    nc = t // rows
    if rows < SSD_CHUNK:
        dt = jnp.pad(dt, ((0, 0), (0, SSD_CHUNK - rows), (0, 0)))
    lane_pad = lambda v: jnp.pad(v.astype(F32), (0, LANES - v.shape[0])).reshape(1, LANES)
    tile = lambda n: pl.BlockSpec((None, rows, n), lambda bi, ci: (bi, ci, 0))
    const = lambda shape: pl.BlockSpec(shape, lambda bi, ci: (0,) * len(shape))
    state_spec = pl.BlockSpec((None, SSM_HEADS, SSM_HEADDIM, SSM_STATE), lambda bi, ci: (bi, 0, 0, 0))
    in_specs = [tile(SSM_INNER), tile(CONV_DIM), pl.BlockSpec((None, SSD_CHUNK, LANES), lambda bi, ci: (bi, ci, 0))]
    args = [z, xbc, dt]
    scratch = []
    if not activated:
        in_specs += [const((SSM_CONV, CONV_DIM)), const((1, CONV_DIM))]
        args += [conv[0], conv[1].reshape(1, CONV_DIM)]
        scratch += [pltpu.VMEM((SUBLANES + SSD_CHUNK, CONV_DIM), F32),
                    pltpu.VMEM((SSD_CHUNK, CONV_DIM), F32)]
    in_specs += [const((1, LANES)), const((1, LANES)), const((1, SSM_INNER)), const((1, SSM_INNER))]
    args += [lane_pad(dt_bias), lane_pad(a_log),
             jnp.repeat(d_skip.astype(F32), SSM_HEADDIM).reshape(1, SSM_INNER), gate_norm.reshape(1, SSM_INNER)]
    if has_init:
        in_specs += [pl.BlockSpec((None, SUBLANES, CONV_DIM), lambda bi, ci: (bi, 0, 0)), state_spec]
        args += [jnp.pad(conv0, ((0, 0), (SUBLANES - (SSM_CONV - 1), 0), (0, 0))), ssm0]
    scratch += [pltpu.VMEM((SSD_CHUNK, SSM_INNER), F32),
                pltpu.VMEM((SSM_HEADS // 2, SSM_STATE, 2 * SSM_HEADDIM), F32)]
    y_dtype = BF16 if rows % 16 == 0 else F32
    return pl.pallas_call(
        functools.partial(_ssd_kernel, rows=rows, has_init=has_init, activated=activated),
        out_shape=(jax.ShapeDtypeStruct((b, t, SSM_INNER), y_dtype),
                   jax.ShapeDtypeStruct((b, SSM_HEADS, SSM_HEADDIM, SSM_STATE), F32)),
        grid=(b, nc),
        in_specs=in_specs,
        out_specs=(tile(SSM_INNER), state_spec),
        scratch_shapes=scratch,
        compiler_params=_params("parallel", "arbitrary"),
        name="ssd_chunk",
    )(*args)


def kernel(x_prompt, x_sample, cache_k, cache_v, page_table, state_conv, state_ssm, attn_norm, w_qkv, q_norm, k_norm, w_o, ssm_norm, w_in, conv_w, conv_b, dt_bias, a_log, d_skip, gate_norm, w_out, ffn_norm, w_gate, w_up, w_down):
    b, s, d = x_prompt.shape
    db, t, _ = x_sample.shape
    past = page_table.shape[1] * PAGE_SIZE
    tm_p, tm_s = 512, db * t
    bf = lambda w: w.astype(BF16)

    w_qkv_bf, w_o_bf = bf(w_qkv[0]), bf(w_o[0])
    ffn = (ffn_norm, bf(w_gate), bf(w_up), bf(w_down))
    ffn0, ffn1 = ffn + (0,), ffn + (1,)
    q_p, k_p, v_p = _qkv(x_prompt, attn_norm[0], w_qkv_bf, q_norm[0], k_norm[0], jnp.arange(s), tm_p)
    attn_p = _moba_prompt(q_p, k_p, v_p)
    pos_s = past + jnp.tile(jnp.arange(t), db)
    xs_rows = x_sample.reshape(1, db * t, d)
    qkv_s = _qkv(xs_rows, attn_norm[0], w_qkv_bf, q_norm[0], k_norm[0], pos_s, tm_s)
    q_s, k_s, v_s = (jnp.transpose(a.reshape(N_HEADS, db, t, HEAD_DIM), (1, 0, 2, 3)) for a in qkv_s)
    attn_s = _moba_sample(q_s, k_s, v_s, cache_k[0], cache_v[0], page_table)
    h_p = _proj_ffn(attn_p, x_prompt.reshape(b * s, d), w_o_bf, *ffn0, tm_p)
    h_s = _proj_ffn(attn_s.reshape(db * t, d), x_sample.reshape(db * t, d), w_o_bf, *ffn0, tm_s)

    w_in1 = w_in[0]
    wz_bf = bf(w_in1[:, :SSM_INNER])
    wx_bf = bf(w_in1[:, SSM_INNER:SSM_INNER + CONV_DIM])
    wdt_bf = bf(jnp.pad(w_in1[:, SSM_INNER + CONV_DIM:], ((0, 0), (0, LANES - SSM_HEADS))))
    w_out_bf = bf(w_out[0])
    conv = (conv_w[0], conv_b[0])
    ssd_w = (dt_bias[0], a_log[0], d_skip[0], gate_norm[0])
    zs_p, act_p, dt_p, tail_p = _inproj_conv(h_p, ssm_norm[0], wz_bf, wx_bf, wdt_bf, *conv, 256, s)
    z_s, xbc_s, dt_s = _inproj(h_s, ssm_norm[0], wz_bf, wx_bf, wdt_bf, tm_s)
    xbc_s = xbc_s.reshape(db, t, CONV_DIM)
    y_p, ssm_p = _ssd(zs_p.reshape(b, s, SSM_INNER), act_p.reshape(b, s, CONV_DIM), dt_p.reshape(b, s, LANES), *ssd_w)
    y_s, ssm_s = _ssd(z_s.reshape(db, t, SSM_INNER), xbc_s, dt_s.reshape(db, t, LANES), *ssd_w,
                      conv=conv, conv0=state_conv[0], ssm0=state_ssm[0])
    out_p = _proj_ffn(y_p.reshape(b * s, SSM_INNER), h_p, w_out_bf, *ffn1, tm_p)
    out_s = _proj_ffn(bf(y_s).reshape(db * t, SSM_INNER), h_s, w_out_bf, *ffn1, tm_s)

    keep = SSM_CONV - 1
    conv_p = tail_p[:, SUBLANES - keep:]
    conv_s = jnp.concatenate([state_conv[0], xbc_s], axis=1)[:, t:]
    return (out_p.reshape(b, s, d), out_s.reshape(db, t, d), k_p[None], v_p[None], k_s[None], v_s[None],
            conv_p[None], ssm_p[None], conv_s[None], ssm_s[None])
```

```python
import functools

import jax
import jax.numpy as jnp
from jax import lax
from jax.experimental import pallas as pl
from jax.experimental.pallas import tpu as pltpu

F32 = jnp.float32
BF16 = jnp.bfloat16

EPS = 1e-6
N_HEADS = 8
HEAD_DIM = 128
ROT_DIM = 32
ROPE_THETA = 500000.0
MOBA_BLOCK = 256
MOBA_TOPK = 3
PAGE_SIZE = 128
SSM_INNER = 2048
SSM_HEADDIM = 64
SSM_HEADS = 32
SSM_GROUPS = 4
SSM_HPG = SSM_HEADS // SSM_GROUPS
SSM_STATE = 128
SSM_CONV = 4
CONV_DIM = SSM_INNER + 2 * SSM_GROUPS * SSM_STATE
SSD_CHUNK = 128

LANES = 128
SUBLANES = 8
VMEM_LIMIT_BYTES = 56 * 2**20
MOBA_UNROLL = 2
MOBA_HEADS_PER_STEP = 4
PAGES_PER_STEP = 32
SAMPLE_Q_ROWS = 8

LOG2_E = 1.4426950408889634

_NT = (((1,), (1,)), ((), ()))


def _params(*semantics):
    return pltpu.CompilerParams(dimension_semantics=semantics, vmem_limit_bytes=VMEM_LIMIT_BYTES)


def _rms(x, g):
    return x * lax.rsqrt(jnp.mean(x * x, axis=-1, keepdims=True) + EPS) * g


def _silu(x):
    return x / (1.0 + jnp.exp(-x))


def _qkv_kernel(x_ref, g_ref, w_ref, qg_ref, kg_ref, cos_ref, sa_ref, sb_ref, q_ref, k_ref, v_ref):
    n = _rms(x_ref[...], g_ref[...]).astype(BF16)
    cos, sa, sb = cos_ref[...], sa_ref[...], sb_ref[...]
    d = N_HEADS * HEAD_DIM
    for part, (o_ref, gain_ref) in enumerate(((q_ref, qg_ref), (k_ref, kg_ref), (v_ref, None))):
        y = jnp.dot(n, w_ref[:, part * d:(part + 1) * d], preferred_element_type=F32)
        for h in range(N_HEADS):
            yh = y[:, h * HEAD_DIM:(h + 1) * HEAD_DIM]
            if gain_ref is not None:
                yh = _rms(yh, gain_ref[...])
                yh = (yh * cos + pltpu.roll(yh, HEAD_DIM - ROT_DIM // 2, 1) * sa
                      + pltpu.roll(yh, ROT_DIM // 2, 1) * sb)
            o_ref[h] = yh


def _rope_tables(pos):
    half = ROT_DIM // 2
    inv = ROPE_THETA ** (-jnp.arange(half, dtype=F32) * 2.0 / ROT_DIM)
    ang = pos.astype(F32)[:, None] * inv[None, :]
    cos, sin = jnp.cos(ang), jnp.sin(ang)
    t = pos.shape[0]
    cos_t = jnp.concatenate([cos, cos, jnp.ones((t, HEAD_DIM - ROT_DIM), F32)], axis=1)
    sa = jnp.concatenate([-sin, jnp.zeros((t, HEAD_DIM - half), F32)], axis=1)
    sb = jnp.concatenate([jnp.zeros((t, half), F32), sin, jnp.zeros((t, HEAD_DIM - ROT_DIM), F32)], axis=1)
    return cos_t, sa, sb


def _qkv(x, g, w_bf, qg, kg, pos, tm):
    b, t, d = x.shape
    cos_t, sa, sb = _rope_tables(pos)
    row = lambda v: v.reshape(1, -1)
    out = jax.ShapeDtypeStruct((b, N_HEADS, t, HEAD_DIM), F32)
    tab_spec = pl.BlockSpec((tm, HEAD_DIM), lambda bi, i: (i, 0))
    out_spec = pl.BlockSpec((None, N_HEADS, tm, HEAD_DIM), lambda bi, i: (bi, 0, i, 0))
    const = lambda shape: pl.BlockSpec(shape, lambda bi, i: (0,) * len(shape))
    return pl.pallas_call(
        _qkv_kernel,
        out_shape=(out, out, out),
        grid=(b, t // tm),
        in_specs=[pl.BlockSpec((None, tm, d), lambda bi, i: (bi, i, 0)),
                  const((1, d)), const(w_bf.shape), const((1, HEAD_DIM)), const((1, HEAD_DIM)),
                  tab_spec, tab_spec, tab_spec],
        out_specs=(out_spec, out_spec, out_spec),
        compiler_params=_params("parallel", "parallel"),
        name="qkv_norm_rope",
    )(x, row(g), w_bf, row(qg), row(kg), cos_t, sa, sb)


def _topk_rank(entries, gate, index):
    rank = jnp.zeros(gate.shape, F32)
    for j, c in enumerate(entries):
        ahead = (c > gate) | ((c == gate) & (j < index))
        rank = rank + jnp.where(ahead, 1.0, 0.0)
    return rank


def _moba_prompt_kernel(q_ref, k_ref, v_ref, o_ref, kb_ref, vt_ref, km_ref, bias_ref):
    i = pl.program_id(2)
    heads = q_ref.shape[0]
    nb = k_ref.shape[1] // MOBA_BLOCK
    scale = HEAD_DIM ** -0.5

    @pl.when(i == 0)
    def _():
        kb_ref[...] = k_ref[...].astype(BF16)
        km_ref[...] = jnp.zeros(km_ref.shape, F32)
        for h in range(heads):
            for j in range(nb):
                rows = slice(j * MOBA_BLOCK, (j + 1) * MOBA_BLOCK)
                vt_ref[h, j] = v_ref[h, rows, :].T.astype(BF16)
                km_ref[h, j:j + 1, :] = jnp.mean(k_ref[h, rows, :], axis=0, keepdims=True)

    qbs = []
    for h in range(heads):
        qf = q_ref[h]
        qbs.append((qf * (scale * LOG2_E)).astype(BF16))
        gate = lax.dot_general(km_ref[h], qf, _NT, precision=lax.Precision.HIGHEST,
                               preferred_element_type=F32)
        blk = lax.broadcasted_iota(jnp.int32, gate.shape, 0)
        past = blk < i
        gate = jnp.where(past, gate, -jnp.inf)
        rank = _topk_rank([gate[j:j + 1, :] for j in range(nb)], gate, blk)
        bias_ref[h] = jnp.where((rank < MOBA_TOPK) & past, 0.0, -jnp.inf)

    def scores(h, j):
        start = j * MOBA_BLOCK
        if not isinstance(j, int):
            start = pl.multiple_of(start, MOBA_BLOCK)
        kj = kb_ref[h, pl.ds(start, MOBA_BLOCK), :]
        return lax.dot_general(kj, qbs[h], _NT, preferred_element_type=F32)

    def attend(n_lead):
        for h in range(heads):
            s = scores(h, i)
            ki = lax.broadcasted_iota(jnp.int32, s.shape, 0)
            qi = lax.broadcasted_iota(jnp.int32, s.shape, 1)
            s_own = jnp.where(ki <= qi, s, -jnp.inf)
            lead = [(j, scores(h, j), bias_ref[h, j:j + 1, :]) for j in range(n_lead)]
            m = jnp.max(s_own, axis=0, keepdims=True)
            for _, s, bias in lead:
                m = jnp.maximum(m, jnp.max(s, axis=0, keepdims=True) + bias)
            blocks = [(i, s_own, m)] + [(j, s, m - bias) for j, s, bias in lead]
            l, acc = 0.0, 0.0
            for j, s, shift in blocks:
                p = jnp.exp2(s - shift)
                l = l + jnp.sum(p, axis=0, keepdims=True)
                acc = acc + jnp.dot(vt_ref[h, j], p.astype(BF16), preferred_element_type=F32)
            o_ref[h * HEAD_DIM:(h + 1) * HEAD_DIM, :] = (acc / l).astype(o_ref.dtype)

    groups = pl.cdiv(i, MOBA_UNROLL)
    for v in range(nb // MOBA_UNROLL + 1):
        pl.when(groups == v)(functools.partial(attend, v * MOBA_UNROLL))


def _moba_prompt(q, k, v):
    b, h, s, d = q.shape
    hs = MOBA_HEADS_PER_STEP
    assert s % (MOBA_BLOCK * MOBA_UNROLL) == 0 and h % hs == 0
    tq = MOBA_BLOCK
    nb = s // MOBA_BLOCK
    nb_pad = pl.cdiv(nb, SUBLANES) * SUBLANES
    full = pl.BlockSpec((None, hs, s, d), lambda bi, hi, i: (bi, hi, 0, 0))
    return pl.pallas_call(
        _moba_prompt_kernel,
        out_shape=jax.ShapeDtypeStruct((b, h * d, s), BF16),
        grid=(b, h // hs, s // tq),
        in_specs=[pl.BlockSpec((None, hs, tq, d), lambda bi, hi, i: (bi, hi, i, 0)), full, full],
        out_specs=pl.BlockSpec((None, hs * d, tq), lambda bi, hi, i: (bi, hi, i)),
        scratch_shapes=[pltpu.VMEM((hs, s, d), BF16),
                        pltpu.VMEM((hs, nb, d, MOBA_BLOCK), BF16),
                        pltpu.VMEM((hs, nb_pad, d), F32),
                        pltpu.VMEM((hs, nb_pad, tq), F32)],
        compiler_params=_params("parallel", "parallel", "arbitrary"),
        name="moba_prompt",
    )(q, k, v)


def _sample_scores_kernel(pt_ref, q_ref, *refs):
    del pt_ref
    k_refs, s_ref = refs[:-1], refs[-1]
    for r, k_ref in enumerate(k_refs):
        for h in range(N_HEADS):
            s = lax.dot_general(q_ref[h].astype(BF16), k_ref[h].astype(BF16), _NT, preferred_element_type=F32)
            s_ref[h, :, r * PAGE_SIZE:(r + 1) * PAGE_SIZE] = s


def _sample_select_kernel(q_ref, kn_ref, vn_ref, s_ref, pc_ref, sel_ref, ol_ref, *, n_new, n_full):
    scale = HEAD_DIM ** -0.5
    rows = s_ref.shape[1]
    n_top = min(MOBA_TOPK, n_full)
    lane = lax.broadcasted_iota(jnp.int32, (rows, LANES), 1)
    lane_f = lane.astype(F32)
    row = lax.broadcasted_iota(jnp.int32, (rows, 1), 0)
    for h in range(N_HEADS):
        s = s_ref[h]
        blocks = [s[:, j * MOBA_BLOCK:(j + 1) * MOBA_BLOCK] for j in range(n_full)]
        cols = [jnp.sum(blk, axis=1, keepdims=True) * (1.0 / MOBA_BLOCK) for blk in blocks]
        gate = jnp.full((rows, LANES), -jnp.inf, F32)
        for j, c in enumerate(cols):
            gate = jnp.where(lane == j, c, gate)
        rank = _topk_rank(cols, gate, lane)
        picks = [jnp.sum(jnp.where((rank == r) & (lane < n_full), lane_f, 0.0), axis=1, keepdims=True)
                 for r in range(n_top)]
        s_sel = []
        for pick in picks:
            g = jnp.zeros((rows, MOBA_BLOCK), F32)
            for j, blk in enumerate(blocks):
                g = g + jnp.where(pick == j, blk, 0.0)
            s_sel.append(g * scale)
        q = q_ref[h]
        s_new = []
        for u in range(n_new):
            su = jnp.sum(q * kn_ref[h, u:u + 1, :], axis=1, keepdims=True) * scale
            s_new.append(jnp.where(row >= u, su, -jnp.inf))
        m = functools.reduce(jnp.maximum, [jnp.max(g, axis=1, keepdims=True) for g in s_sel] + s_new)
        p_sel = [jnp.exp(g - m) for g in s_sel]
        p_new = [jnp.exp(su - m) for su in s_new]
        denom = functools.reduce(jnp.add, [jnp.sum(p, axis=1, keepdims=True) for p in p_sel] + p_new)
        ids = jnp.zeros((rows, LANES), F32)
        for r in range(MOBA_TOPK):
            cols_r = slice(r * MOBA_BLOCK, (r + 1) * MOBA_BLOCK)
            if r < n_top:
                pc_ref[h, :, cols_r] = p_sel[r] / denom
                ids = jnp.where(lane == r, picks[r], ids)
            else:
                pc_ref[h, :, cols_r] = jnp.zeros((rows, MOBA_BLOCK), F32)
        sel_ref[h] = ids.astype(jnp.int32)
        o = jnp.zeros((rows, HEAD_DIM), F32)
        for u, pu in enumerate(p_new):
            o = o + (pu / denom) * vn_ref[h, u:u + 1, :]
        ol_ref[h] = o


def _sample_pv_kernel(pt_ref, sel_ref, pc_ref, ol_ref, v_hbm, o_ref, vbuf, sems, *, n_new):
    b = pl.program_id(0)
    n_seq = pl.num_programs(0)
    rows = pc_ref.shape[1]
    ppb = MOBA_BLOCK // PAGE_SIZE
    row = lax.broadcasted_iota(jnp.int32, (rows, MOBA_BLOCK), 0)

    def copies(seq, h):
        out = []
        for t in range(n_new):
            for r in range(MOBA_TOPK):
                k = t * MOBA_TOPK + r
                blk = sel_ref[seq, (h * n_new + t) * MOBA_TOPK + r]
                for half in range(ppb):
                    page = pt_ref[seq, blk * ppb + half]
                    out.append(pltpu.make_async_copy(
                        v_hbm.at[page, h], vbuf.at[h % 2, k, pl.ds(half * PAGE_SIZE, PAGE_SIZE), :], sems.at[h % 2]))
        return out

    @pl.when(b == 0)
    def _():
        for c in copies(b, 0):
            c.start()

    for h in range(N_HEADS):
        if h + 1 < N_HEADS:
            for c in copies(b, h + 1):
                c.start()
        else:
            @pl.when(b + 1 < n_seq)
            def _():
                for c in copies(b + 1, 0):
                    c.start()
        for c in copies(b, h):
            c.wait()
        acc = ol_ref[h]
        for t in range(n_new):
            for r in range(MOBA_TOPK):
                p = jnp.where(row == t, pc_ref[h, :, r * MOBA_BLOCK:(r + 1) * MOBA_BLOCK], 0.0).astype(BF16)
                acc = acc + jnp.dot(p, vbuf[h % 2, t * MOBA_TOPK + r].astype(BF16), preferred_element_type=F32)
        o_ref[h] = acc


def _page_specs(n):
    def spec(r):
        return pl.BlockSpec((None, N_HEADS, PAGE_SIZE, HEAD_DIM),
                            lambda bi, g, pt: (pt[bi, g * n + r], 0, 0, 0))
    return [spec(r) for r in range(n)]


def _moba_sample(q, k_new, v_new, cache_k, cache_v, page_table):
    db, h, t, d = q.shape
    n_pages = page_table.shape[1]
    assert (n_pages * PAGE_SIZE) % MOBA_BLOCK == 0, "cached keys must end on a MoBA block boundary"
    assert n_pages % PAGES_PER_STEP == 0 and t <= SAMPLE_Q_ROWS and h % 2 == 0
    n_full = n_pages * PAGE_SIZE // MOBA_BLOCK
    n_keys = n_pages * PAGE_SIZE
    rows = SAMPLE_Q_ROWS
    steps = n_pages // PAGES_PER_STEP
    pad = lambda a: jnp.pad(a, ((0, 0), (0, 0), (0, rows - t), (0, 0)))
    qp, knp, vnp = pad(q), pad(k_new), pad(v_new)
    per_seq = pl.BlockSpec((None, h, rows, d), lambda bi, g, pt: (bi, 0, 0, 0))
    keys_spec = pl.BlockSpec((None, h, rows, PAGES_PER_STEP * PAGE_SIZE), lambda bi, g, pt: (bi, 0, 0, g))

    scores = pl.pallas_call(
        _sample_scores_kernel,
        out_shape=jax.ShapeDtypeStruct((db, h, rows, n_keys), F32),
        grid_spec=pltpu.PrefetchScalarGridSpec(
            num_scalar_prefetch=1, grid=(db, steps),
            in_specs=[per_seq] + _page_specs(PAGES_PER_STEP), out_specs=keys_spec),
        compiler_params=_params("parallel", "arbitrary"),
        name="moba_sample_scores",
    )(page_table, qp, *([cache_k] * PAGES_PER_STEP))

    seq4 = lambda last: pl.BlockSpec((None, h, rows, last), lambda bi: (bi, 0, 0, 0))
    probs, sel, o_new = pl.pallas_call(
        functools.partial(_sample_select_kernel, n_new=t, n_full=n_full),
        out_shape=(jax.ShapeDtypeStruct((db, h, rows, MOBA_TOPK * MOBA_BLOCK), F32),
                   jax.ShapeDtypeStruct((db, h, rows, LANES), jnp.int32),
                   jax.ShapeDtypeStruct((db, h, rows, d), F32)),
        grid=(db,),
        in_specs=[seq4(d), seq4(d), seq4(d), seq4(n_keys)],
        out_specs=(seq4(MOBA_TOPK * MOBA_BLOCK), seq4(LANES), seq4(d)),
        compiler_params=_params("parallel"),
        name="moba_sample_select",
    )(qp, knp, vnp, scores)
    sel = sel[:, :, :t, :MOBA_TOPK].reshape(db, h * t * MOBA_TOPK)

    seq2 = lambda last: pl.BlockSpec((None, h, rows, last), lambda bi, pt, sl: (bi, 0, 0, 0))
    out = pl.pallas_call(
        functools.partial(_sample_pv_kernel, n_new=t),
        out_shape=jax.ShapeDtypeStruct((db, h, rows, d), F32),
        grid_spec=pltpu.PrefetchScalarGridSpec(
            num_scalar_prefetch=2, grid=(db,),
            in_specs=[seq2(MOBA_TOPK * MOBA_BLOCK), seq2(d), pl.BlockSpec(memory_space=pl.ANY)],
            out_specs=seq2(d),
            scratch_shapes=[pltpu.VMEM((2, t * MOBA_TOPK, MOBA_BLOCK, d), F32),
                            pltpu.SemaphoreType.DMA((2,))]),
        compiler_params=_params("arbitrary"),
        name="moba_sample_pv",
    )(page_table, sel, probs, o_new, cache_v)
    return jnp.transpose(out[:, :, :t], (0, 2, 1, 3)).reshape(db, t, h * d).astype(BF16)


def _proj_ffn_kernel(a_ref, x_ref, wp_ref, g_ref, wg_ref, wu_ref, wd_ref, o_ref, *, chunks, a_transposed):
    a = a_ref[...].T if a_transposed else a_ref[...]
    hid = x_ref[...] + jnp.dot(a, wp_ref[...], preferred_element_type=F32)
    n = _rms(hid, g_ref[...]).astype(BF16)
    width = wg_ref.shape[1] // chunks
    acc = hid
    for c in range(chunks):
        cols = slice(c * width, (c + 1) * width)
        gate = jnp.dot(n, wg_ref[:, cols], preferred_element_type=F32)
        up = jnp.dot(n, wu_ref[:, cols], preferred_element_type=F32)
        act = (_silu(gate) * up).astype(BF16)
        acc = acc + jnp.dot(act, wd_ref[cols, :], preferred_element_type=F32)
    o_ref[...] = acc


def _proj_ffn(a_bf, x, wp_bf, g, wg_bf, wu_bf, wd_bf, layer, tm, chunks=2):
    m, d = x.shape
    stacked = lambda w: pl.BlockSpec((None,) + w.shape[1:], lambda i: (layer, 0, 0), pipeline_mode=pl.Buffered(1))
    a_transposed = a_bf.ndim == 3
    if a_transposed:
        nb, ka, per = a_bf.shape
        assert nb * per == m and per % tm == 0
        a_spec = pl.BlockSpec((None, ka, tm), lambda i: (i // (per // tm), 0, i % (per // tm)))
    else:
        ka = a_bf.shape[1]
        a_spec = pl.BlockSpec((tm, ka), lambda i: (i, 0))
    const = lambda shape: pl.BlockSpec(shape, lambda i: (0, 0), pipeline_mode=pl.Buffered(1))
    return pl.pallas_call(
        functools.partial(_proj_ffn_kernel, chunks=chunks, a_transposed=a_transposed),
        out_shape=jax.ShapeDtypeStruct((m, d), F32),
        grid=(m // tm,),
        in_specs=[a_spec, pl.BlockSpec((tm, d), lambda i: (i, 0)),
                  const(wp_bf.shape), const((1, d)), stacked(wg_bf), stacked(wu_bf), stacked(wd_bf)],
        out_specs=pl.BlockSpec((tm, d), lambda i: (i, 0)),
        compiler_params=_params("parallel"),
        name="proj_ffn",
    )(a_bf, x, wp_bf, g[layer].reshape(1, d), wg_bf, wu_bf, wd_bf)


def _inproj_kernel(x_ref, g_ref, wz_ref, wx_ref, wdt_ref, z_ref, xbc_ref, dt_ref):
    n = _rms(x_ref[...], g_ref[...]).astype(BF16)
    z_ref[...] = jnp.dot(n, wz_ref[...], preferred_element_type=F32)
    xbc_ref[...] = jnp.dot(n, wx_ref[...], preferred_element_type=F32)
    dt_ref[...] = jnp.dot(n, wdt_ref[...], preferred_element_type=F32)


def _inproj(x, g, wz_bf, wx_bf, wdt_bf, tm):
    m, d = x.shape
    const = lambda shape: pl.BlockSpec(shape, lambda i: (0, 0), pipeline_mode=pl.Buffered(1))
    tile = lambda n: pl.BlockSpec((tm, n), lambda i: (i, 0))
    return pl.pallas_call(
        _inproj_kernel,
        out_shape=(jax.ShapeDtypeStruct((m, SSM_INNER), F32), jax.ShapeDtypeStruct((m, CONV_DIM), F32),
                   jax.ShapeDtypeStruct((m, LANES), F32)),
        grid=(m // tm,),
        in_specs=[tile(d), const((1, d)), const(wz_bf.shape), const(wx_bf.shape), const(wdt_bf.shape)],
        out_specs=(tile(SSM_INNER), tile(CONV_DIM), tile(LANES)),
        compiler_params=_params("parallel"),
        name="ssm_in_proj",
    )(x, g.reshape(1, d), wz_bf, wx_bf, wdt_bf)


def _ssd_kernel(*refs, rows, has_init):
    if has_init:
        (z_ref, xbc_ref, dt_ref, cw_ref, cb_ref, dtb_ref, alog_ref, dsk_ref, gn_ref, conv0_ref, ssm0_ref,
         y_ref, hout_ref, buf_ref, act_ref, ysc_ref, ht_ref) = refs
    else:
        (z_ref, xbc_ref, dt_ref, cw_ref, cb_ref, dtb_ref, alog_ref, dsk_ref, gn_ref,
         y_ref, hout_ref, buf_ref, act_ref, ysc_ref, ht_ref) = refs
    L = SSD_CHUNK
    P, N, G = SSM_HEADDIM, SSM_STATE, SSM_GROUPS
    PAIRS = SSM_HEADS // 2
    PPG = PAIRS // G
    GW = SSM_HPG * P
    assert L == N == 2 * P == LANES
    c = pl.program_id(1)
    last = pl.num_programs(1) - 1
    tail = SUBLANES - (SSM_CONV - 1)

    @pl.when(c == 0)
    def _():
        buf_ref[...] = jnp.zeros(buf_ref.shape, F32)
        if has_init:
            buf_ref[0:SUBLANES, :] = conv0_ref[...]
            for pair in range(PAIRS):
                two = jnp.concatenate([ssm0_ref[2 * pair], ssm0_ref[2 * pair + 1]], axis=0)
                ht_ref[pair] = two.T
        else:
            ht_ref[...] = jnp.zeros(ht_ref.shape, F32)

    buf_ref[SUBLANES:SUBLANES + rows, :] = xbc_ref[...]
    conv = cb_ref[...] + buf_ref[tail:tail + L, :] * cw_ref[0:1, :]
    for w in range(1, SSM_CONV):
        conv = conv + buf_ref[tail + w:tail + w + L, :] * cw_ref[w:w + 1, :]
    act_ref[...] = _silu(conv)
    if rows == L:
        buf_ref[0:SUBLANES, :] = buf_ref[L:L + SUBLANES, :]

    x = dt_ref[...] + dtb_ref[...]
    dt = jnp.maximum(x, 0.0) + jnp.log1p(jnp.exp(-jnp.abs(x)))
    if rows < L:
        dt = jnp.where(lax.broadcasted_iota(jnp.int32, dt.shape, 0) < rows, dt, 0.0)
    a = dt * (-jnp.exp(alog_ref[...]))
    ti = lax.broadcasted_iota(jnp.int32, (L, L), 0)
    si = lax.broadcasted_iota(jnp.int32, (L, L), 1)
    causal = ti >= si
    a_cs = jnp.dot(jnp.where(causal, 1.0, 0.0), a, precision=lax.Precision.HIGHEST,
                   preferred_element_type=F32)
    a_tot = a_cs[L - 1:L, :]
    ea_tot = jnp.exp(a_tot)
    a_cs_t = a_cs.T
    dt_t = dt.T
    dtde_t = (dt * jnp.exp(a_tot - a_cs)).T
    first = lax.broadcasted_iota(jnp.int32, (L, LANES), 1) < P

    for g in range(G):
        bg = act_ref[:, SSM_INNER + g * N:SSM_INNER + (g + 1) * N]
        cg = act_ref[:, SSM_INNER + G * N + g * N:SSM_INNER + G * N + (g + 1) * N]
        cb = lax.dot_general(cg.astype(BF16), bg.astype(BF16), _NT, preferred_element_type=F32)
        bg_t = bg.T
        for pr in range(PPG):
            pair = g * PPG + pr
            lanes = slice(pair * LANES, (pair + 1) * LANES)
            xs = act_ref[:, lanes]
            ht = ht_ref[pair]
            y = dsk_ref[:, lanes] * xs
            upd = jnp.zeros((N, LANES), F32)
            for second in range(2):
                e = 2 * pair + second
                mine = lambda v: (jnp.where(first, 0.0, v) if second else jnp.where(first, v, 0.0)).astype(BF16)
                xs_e, ht_e = mine(xs), mine(ht)
                acs = jnp.broadcast_to(a_cs[:, e:e + 1], (L, L))
                decay = jnp.exp(jnp.where(causal, acs - a_cs_t[e:e + 1, :], -jnp.inf))
                w = (cb * decay * dt_t[e:e + 1, :]).astype(BF16)
                c_in = (cg * jnp.exp(acs)).astype(BF16)
                y = (y + jnp.dot(w, xs_e, preferred_element_type=F32)
                     + jnp.dot(c_in, ht_e, preferred_element_type=F32))
                b_out = (bg_t * dtde_t[e:e + 1, :]).astype(BF16)
                upd = upd + jnp.dot(b_out, xs_e, preferred_element_type=F32)
            ysc_ref[:, lanes] = y
            keep = jnp.where(first[0:1, :], ea_tot[:, 2 * pair:2 * pair + 1], ea_tot[:, 2 * pair + 1:2 * pair + 2])
            ht_ref[pair] = keep * ht + upd

    z_in = z_ref[...]
    for g in range(G):
        lanes = slice(g * GW, (g + 1) * GW)
        yg = ysc_ref[0:rows, lanes] * _silu(z_in[:, lanes])
        y_ref[:, lanes] = _rms(yg, gn_ref[:, lanes]).astype(y_ref.dtype)

    @pl.when(c == last)
    def _():
        for pair in range(PAIRS):
            two = ht_ref[pair].T
            hout_ref[2 * pair] = two[0:P]
            hout_ref[2 * pair + 1] = two[P:2 * P]


def _ssd(z, xbc, dt, conv_w, conv_b, dt_bias, a_log, d_skip, gate_norm, conv0=None, ssm0=None):
    b, t, _ = z.shape
    has_init = conv0 is not None
    rows = min(SSD_CHUNK, t)
    assert t % rows == 0 and (rows == SSD_CHUNK or t == rows)
    nc = t // rows
    if rows < SSD_CHUNK:
        dt = jnp.pad(dt, ((0, 0), (0, SSD_CHUNK - rows), (0, 0)))
    lane_pad = lambda v: jnp.pad(v.astype(F32), (0, LANES - v.shape[0])).reshape(1, LANES)
    tile = lambda n: pl.BlockSpec((None, rows, n), lambda bi, ci: (bi, ci, 0))
    const = lambda shape: pl.BlockSpec(shape, lambda bi, ci: (0,) * len(shape))
    state_spec = pl.BlockSpec((None, SSM_HEADS, SSM_HEADDIM, SSM_STATE), lambda bi, ci: (bi, 0, 0, 0))
    in_specs = [tile(SSM_INNER), tile(CONV_DIM), pl.BlockSpec((None, SSD_CHUNK, LANES), lambda bi, ci: (bi, ci, 0)),
                const((SSM_CONV, CONV_DIM)), const((1, CONV_DIM)),
                const((1, LANES)), const((1, LANES)), const((1, SSM_INNER)), const((1, SSM_INNER))]
    args = [z, xbc, dt, conv_w, conv_b.reshape(1, CONV_DIM), lane_pad(dt_bias), lane_pad(a_log),
            jnp.repeat(d_skip.astype(F32), SSM_HEADDIM).reshape(1, SSM_INNER), gate_norm.reshape(1, SSM_INNER)]
    if has_init:
        in_specs += [pl.BlockSpec((None, SUBLANES, CONV_DIM), lambda bi, ci: (bi, 0, 0)), state_spec]
        args += [jnp.pad(conv0, ((0, 0), (SUBLANES - (SSM_CONV - 1), 0), (0, 0))), ssm0]
    scratch = [pltpu.VMEM((SUBLANES + SSD_CHUNK, CONV_DIM), F32),
               pltpu.VMEM((SSD_CHUNK, CONV_DIM), F32),
               pltpu.VMEM((SSD_CHUNK, SSM_INNER), F32),
               pltpu.VMEM((SSM_HEADS // 2, SSM_STATE, 2 * SSM_HEADDIM), F32)]
    y_dtype = BF16 if rows % 16 == 0 else F32
    return pl.pallas_call(
        functools.partial(_ssd_kernel, rows=rows, has_init=has_init),
        out_shape=(jax.ShapeDtypeStruct((b, t, SSM_INNER), y_dtype),
                   jax.ShapeDtypeStruct((b, SSM_HEADS, SSM_HEADDIM, SSM_STATE), F32)),
        grid=(b, nc),
        in_specs=in_specs,
        out_specs=(tile(SSM_INNER), state_spec),
        scratch_shapes=scratch,
        compiler_params=_params("parallel", "arbitrary"),
        name="ssd_chunk",
    )(*args)


def kernel(x_prompt, x_sample, cache_k, cache_v, page_table, state_conv, state_ssm, attn_norm, w_qkv, q_norm, k_norm, w_o, ssm_norm, w_in, conv_w, conv_b, dt_bias, a_log, d_skip, gate_norm, w_out, ffn_norm, w_gate, w_up, w_down):
    b, s, d = x_prompt.shape
    db, t, _ = x_sample.shape
    past = page_table.shape[1] * PAGE_SIZE
    tm_p, tm_s = 512, db * t
    bf = lambda w: w.astype(BF16)

    w_qkv_bf, w_o_bf = bf(w_qkv[0]), bf(w_o[0])
    ffn = (ffn_norm, bf(w_gate), bf(w_up), bf(w_down))
    ffn0, ffn1 = ffn + (0,), ffn + (1,)
    q_p, k_p, v_p = _qkv(x_prompt, attn_norm[0], w_qkv_bf, q_norm[0], k_norm[0], jnp.arange(s), tm_p)
    attn_p = _moba_prompt(q_p, k_p, v_p)
    pos_s = past + jnp.tile(jnp.arange(t), db)
    xs_rows = x_sample.reshape(1, db * t, d)
    qkv_s = _qkv(xs_rows, attn_norm[0], w_qkv_bf, q_norm[0], k_norm[0], pos_s, tm_s)
    q_s, k_s, v_s = (jnp.transpose(a.reshape(N_HEADS, db, t, HEAD_DIM), (1, 0, 2, 3)) for a in qkv_s)
    attn_s = _moba_sample(q_s, k_s, v_s, cache_k[0], cache_v[0], page_table)
    h_p = _proj_ffn(attn_p, x_prompt.reshape(b * s, d), w_o_bf, *ffn0, tm_p)
    h_s = _proj_ffn(attn_s.reshape(db * t, d), x_sample.reshape(db * t, d), w_o_bf, *ffn0, tm_s)

    w_in1 = w_in[0]
    wz_bf = bf(w_in1[:, :SSM_INNER])
    wx_bf = bf(w_in1[:, SSM_INNER:SSM_INNER + CONV_DIM])
    wdt_bf = bf(jnp.pad(w_in1[:, SSM_INNER + CONV_DIM:], ((0, 0), (0, LANES - SSM_HEADS))))
    w_out_bf = bf(w_out[0])
    conv = (conv_w[0], conv_b[0])
    ssd_w = (dt_bias[0], a_log[0], d_skip[0], gate_norm[0])
    z_p, xbc_p, dt_p = _inproj(h_p, ssm_norm[0], wz_bf, wx_bf, wdt_bf, 256)
    z_s, xbc_s, dt_s = _inproj(h_s, ssm_norm[0], wz_bf, wx_bf, wdt_bf, tm_s)
    xbc_p = xbc_p.reshape(b, s, CONV_DIM)
    xbc_s = xbc_s.reshape(db, t, CONV_DIM)
    y_p, ssm_p = _ssd(z_p.reshape(b, s, SSM_INNER), xbc_p, dt_p.reshape(b, s, LANES), *conv, *ssd_w)
    y_s, ssm_s = _ssd(z_s.reshape(db, t, SSM_INNER), xbc_s, dt_s.reshape(db, t, LANES), *conv, *ssd_w,
                      conv0=state_conv[0], ssm0=state_ssm[0])
    out_p = _proj_ffn(y_p.reshape(b * s, SSM_INNER), h_p, w_out_bf, *ffn1, tm_p)
    out_s = _proj_ffn(bf(y_s).reshape(db * t, SSM_INNER), h_s, w_out_bf, *ffn1, tm_s)

    keep = SSM_CONV - 1
    conv_p = xbc_p[:, s - keep:]
    conv_s = jnp.concatenate([state_conv[0], xbc_s], axis=1)[:, t:]
    return (out_p.reshape(b, s, d), out_s.reshape(db, t, d), k_p[None], v_p[None], k_s[None], v_s[None],
            conv_p[None], ssm_p[None], conv_s[None], ssm_s[None])
```

```python
import functools

import jax
import jax.numpy as jnp
from jax import lax
from jax.experimental import pallas as pl
from jax.experimental.pallas import tpu as pltpu

F32 = jnp.float32
BF16 = jnp.bfloat16

EPS = 1e-6
N_HEADS = 8
HEAD_DIM = 128
ROT_DIM = 32
ROPE_THETA = 500000.0
MOBA_BLOCK = 256
MOBA_TOPK = 3
PAGE_SIZE = 128
SSM_INNER = 2048
SSM_HEADDIM = 64
SSM_HEADS = 32
SSM_GROUPS = 4
SSM_HPG = SSM_HEADS // SSM_GROUPS
SSM_STATE = 128
SSM_CONV = 4
CONV_DIM = SSM_INNER + 2 * SSM_GROUPS * SSM_STATE
SSD_CHUNK = 128

LANES = 128
SUBLANES = 8
VMEM_LIMIT_BYTES = 56 * 2**20
MOBA_UNROLL = 2
MOBA_HEADS_PER_STEP = 4
PAGES_PER_STEP = 32
PV_SLOTS = 4
PV_AHEAD = 3
FFN_SLAB = 256
FFN_SLAB_MIN_ROWS = 512
SAMPLE_Q_ROWS = 8

LOG2_E = 1.4426950408889634

_NT = (((1,), (1,)), ((), ()))


def _params(*semantics):
    return pltpu.CompilerParams(dimension_semantics=semantics, vmem_limit_bytes=VMEM_LIMIT_BYTES)


def _rms(x, g):
    return x * lax.rsqrt(jnp.mean(x * x, axis=-1, keepdims=True) + EPS) * g


def _silu(x):
    return x / (1.0 + jnp.exp(-x))


def _qkv_kernel(x_ref, g_ref, w_ref, qg_ref, kg_ref, cos_ref, sa_ref, sb_ref, q_ref, k_ref, v_ref):
    n = _rms(x_ref[...], g_ref[...]).astype(BF16)
    cos, sa, sb = cos_ref[...], sa_ref[...], sb_ref[...]
    d = N_HEADS * HEAD_DIM
    for part, (o_ref, gain_ref) in enumerate(((q_ref, qg_ref), (k_ref, kg_ref), (v_ref, None))):
        y = jnp.dot(n, w_ref[:, part * d:(part + 1) * d], preferred_element_type=F32)
        for h in range(N_HEADS):
            yh = y[:, h * HEAD_DIM:(h + 1) * HEAD_DIM]
            if gain_ref is not None:
                yh = _rms(yh, gain_ref[...])
                yh = (yh * cos + pltpu.roll(yh, HEAD_DIM - ROT_DIM // 2, 1) * sa
                      + pltpu.roll(yh, ROT_DIM // 2, 1) * sb)
            o_ref[h] = yh


def _rope_tables(pos):
    half = ROT_DIM // 2
    inv = ROPE_THETA ** (-jnp.arange(half, dtype=F32) * 2.0 / ROT_DIM)
    ang = pos.astype(F32)[:, None] * inv[None, :]
    cos, sin = jnp.cos(ang), jnp.sin(ang)
    t = pos.shape[0]
    cos_t = jnp.concatenate([cos, cos, jnp.ones((t, HEAD_DIM - ROT_DIM), F32)], axis=1)
    sa = jnp.concatenate([-sin, jnp.zeros((t, HEAD_DIM - half), F32)], axis=1)
    sb = jnp.concatenate([jnp.zeros((t, half), F32), sin, jnp.zeros((t, HEAD_DIM - ROT_DIM), F32)], axis=1)
    return cos_t, sa, sb


def _qkv(x, g, w_bf, qg, kg, pos, tm):
    b, t, d = x.shape
    cos_t, sa, sb = _rope_tables(pos)
    row = lambda v: v.reshape(1, -1)
    out = jax.ShapeDtypeStruct((b, N_HEADS, t, HEAD_DIM), F32)
    tab_spec = pl.BlockSpec((tm, HEAD_DIM), lambda bi, i: (i, 0))
    out_spec = pl.BlockSpec((None, N_HEADS, tm, HEAD_DIM), lambda bi, i: (bi, 0, i, 0))
    const = lambda shape: pl.BlockSpec(shape, lambda bi, i: (0,) * len(shape))
    return pl.pallas_call(
        _qkv_kernel,
        out_shape=(out, out, out),
        grid=(b, t // tm),
        in_specs=[pl.BlockSpec((None, tm, d), lambda bi, i: (bi, i, 0)),
                  const((1, d)), const(w_bf.shape), const((1, HEAD_DIM)), const((1, HEAD_DIM)),
                  tab_spec, tab_spec, tab_spec],
        out_specs=(out_spec, out_spec, out_spec),
        compiler_params=_params("parallel", "parallel"),
        name="qkv_norm_rope",
    )(x, row(g), w_bf, row(qg), row(kg), cos_t, sa, sb)


def _topk_rank(entries, gate, index):
    rank = jnp.zeros(gate.shape, F32)
    for j, c in enumerate(entries):
        ahead = (c > gate) | ((c == gate) & (j < index))
        rank = rank + jnp.where(ahead, 1.0, 0.0)
    return rank


def _moba_prompt_kernel(q_ref, k_ref, v_ref, o_ref, kb_ref, vt_ref, km_ref, bias_ref):
    i = pl.program_id(2)
    heads = q_ref.shape[0]
    nb = k_ref.shape[1] // MOBA_BLOCK
    scale = HEAD_DIM ** -0.5

    @pl.when(i == 0)
    def _():
        kb_ref[...] = k_ref[...].astype(BF16)
        km_ref[...] = jnp.zeros(km_ref.shape, F32)
        for h in range(heads):
            for j in range(nb):
                rows = slice(j * MOBA_BLOCK, (j + 1) * MOBA_BLOCK)
                vt_ref[h, j] = v_ref[h, rows, :].T.astype(BF16)
                km_ref[h, j:j + 1, :] = jnp.mean(k_ref[h, rows, :], axis=0, keepdims=True)

    qbs = []
    for h in range(heads):
        qf = q_ref[h]
        qbs.append((qf * (scale * LOG2_E)).astype(BF16))
        gate = lax.dot_general(km_ref[h], qf, _NT, precision=lax.Precision.HIGHEST,
                               preferred_element_type=F32)
        blk = lax.broadcasted_iota(jnp.int32, gate.shape, 0)
        past = blk < i
        gate = jnp.where(past, gate, -jnp.inf)
        rank = _topk_rank([gate[j:j + 1, :] for j in range(nb)], gate, blk)
        bias_ref[h] = jnp.where((rank < MOBA_TOPK) & past, 0.0, -jnp.inf)

    def scores(h, j):
        start = j * MOBA_BLOCK
        if not isinstance(j, int):
            start = pl.multiple_of(start, MOBA_BLOCK)
        kj = kb_ref[h, pl.ds(start, MOBA_BLOCK), :]
        return lax.dot_general(kj, qbs[h], _NT, preferred_element_type=F32)

    def attend(n_lead):
        for h in range(heads):
            s = scores(h, i)
            ki = lax.broadcasted_iota(jnp.int32, s.shape, 0)
            qi = lax.broadcasted_iota(jnp.int32, s.shape, 1)
            s_own = jnp.where(ki <= qi, s, -jnp.inf)
            lead = [(j, scores(h, j), bias_ref[h, j:j + 1, :]) for j in range(n_lead)]
            m = jnp.max(s_own, axis=0, keepdims=True)
            for _, s, bias in lead:
                m = jnp.maximum(m, jnp.max(s, axis=0, keepdims=True) + bias)
            blocks = [(i, s_own, m)] + [(j, s, m - bias) for j, s, bias in lead]
            l, acc = 0.0, 0.0
            for j, s, shift in blocks:
                p = jnp.exp2(s - shift)
                l = l + jnp.sum(p, axis=0, keepdims=True)
                acc = acc + jnp.dot(vt_ref[h, j], p.astype(BF16), preferred_element_type=F32)
            o_ref[h * HEAD_DIM:(h + 1) * HEAD_DIM, :] = (acc / l).astype(o_ref.dtype)

    groups = pl.cdiv(i, MOBA_UNROLL)
    for v in range(nb // MOBA_UNROLL + 1):
        pl.when(groups == v)(functools.partial(attend, v * MOBA_UNROLL))


def _moba_prompt(q, k, v):
    b, h, s, d = q.shape
    hs = MOBA_HEADS_PER_STEP
    assert s % (MOBA_BLOCK * MOBA_UNROLL) == 0 and h % hs == 0
    tq = MOBA_BLOCK
    nb = s // MOBA_BLOCK
    nb_pad = pl.cdiv(nb, SUBLANES) * SUBLANES
    full = pl.BlockSpec((None, hs, s, d), lambda bi, hi, i: (bi, hi, 0, 0))
    return pl.pallas_call(
        _moba_prompt_kernel,
        out_shape=jax.ShapeDtypeStruct((b, h * d, s), BF16),
        grid=(b, h // hs, s // tq),
        in_specs=[pl.BlockSpec((None, hs, tq, d), lambda bi, hi, i: (bi, hi, i, 0)), full, full],
        out_specs=pl.BlockSpec((None, hs * d, tq), lambda bi, hi, i: (bi, hi, i)),
        scratch_shapes=[pltpu.VMEM((hs, s, d), BF16),
                        pltpu.VMEM((hs, nb, d, MOBA_BLOCK), BF16),
                        pltpu.VMEM((hs, nb_pad, d), F32),
                        pltpu.VMEM((hs, nb_pad, tq), F32)],
        compiler_params=_params("parallel", "parallel", "arbitrary"),
        name="moba_prompt",
    )(q, k, v)


def _sample_scores_kernel(pt_ref, q_ref, *refs):
    del pt_ref
    k_refs, s_ref = refs[:-1], refs[-1]
    for r, k_ref in enumerate(k_refs):
        for h in range(N_HEADS):
            s = lax.dot_general(q_ref[h].astype(BF16), k_ref[h].astype(BF16), _NT, preferred_element_type=F32)
            s_ref[h, :, r * PAGE_SIZE:(r + 1) * PAGE_SIZE] = s


def _sample_select_kernel(q_ref, kn_ref, vn_ref, s_ref, pc_ref, sel_ref, ol_ref, *, n_new, n_full):
    scale = HEAD_DIM ** -0.5
    rows = s_ref.shape[1]
    n_top = min(MOBA_TOPK, n_full)
    lane = lax.broadcasted_iota(jnp.int32, (rows, LANES), 1)
    lane_f = lane.astype(F32)
    row = lax.broadcasted_iota(jnp.int32, (rows, 1), 0)
    for h in range(N_HEADS):
        s = s_ref[h]
        blocks = [s[:, j * MOBA_BLOCK:(j + 1) * MOBA_BLOCK] for j in range(n_full)]
        cols = [jnp.sum(blk, axis=1, keepdims=True) * (1.0 / MOBA_BLOCK) for blk in blocks]
        gate = jnp.full((rows, LANES), -jnp.inf, F32)
        for j, c in enumerate(cols):
            gate = jnp.where(lane == j, c, gate)
        rank = _topk_rank(cols, gate, lane)
        picks = [jnp.sum(jnp.where((rank == r) & (lane < n_full), lane_f, 0.0), axis=1, keepdims=True)
                 for r in range(n_top)]
        s_sel = []
        for pick in picks:
            g = jnp.zeros((rows, MOBA_BLOCK), F32)
            for j, blk in enumerate(blocks):
                g = g + jnp.where(pick == j, blk, 0.0)
            s_sel.append(g * scale)
        q = q_ref[h]
        s_new = []
        for u in range(n_new):
            su = jnp.sum(q * kn_ref[h, u:u + 1, :], axis=1, keepdims=True) * scale
            s_new.append(jnp.where(row >= u, su, -jnp.inf))
        m = functools.reduce(jnp.maximum, [jnp.max(g, axis=1, keepdims=True) for g in s_sel] + s_new)
        p_sel = [jnp.exp(g - m) for g in s_sel]
        p_new = [jnp.exp(su - m) for su in s_new]
        denom = functools.reduce(jnp.add, [jnp.sum(p, axis=1, keepdims=True) for p in p_sel] + p_new)
        ids = jnp.zeros((rows, LANES), F32)
        for r in range(MOBA_TOPK):
            cols_r = slice(r * MOBA_BLOCK, (r + 1) * MOBA_BLOCK)
            if r < n_top:
                pc_ref[h, :, cols_r] = p_sel[r] / denom
                ids = jnp.where(lane == r, picks[r], ids)
            else:
                pc_ref[h, :, cols_r] = jnp.zeros((rows, MOBA_BLOCK), F32)
        sel_ref[h] = ids.astype(jnp.int32)
        o = jnp.zeros((rows, HEAD_DIM), F32)
        for u, pu in enumerate(p_new):
            o = o + (pu / denom) * vn_ref[h, u:u + 1, :]
        ol_ref[h] = o


def _sample_pv_kernel(pt_ref, sel_ref, pc_ref, ol_ref, v_hbm, o_ref, vbuf, sems, *, n_new):
    b = pl.program_id(0)
    n_seq = pl.num_programs(0)
    rows = pc_ref.shape[1]
    ppb = MOBA_BLOCK // PAGE_SIZE
    slots = vbuf.shape[0]
    assert N_HEADS % slots == 0 and PV_AHEAD < slots <= N_HEADS
    row = lax.broadcasted_iota(jnp.int32, (rows, MOBA_BLOCK), 0)

    def copies(seq, h):
        out = []
        for t in range(n_new):
            for r in range(MOBA_TOPK):
                k = t * MOBA_TOPK + r
                blk = sel_ref[seq, (h * n_new + t) * MOBA_TOPK + r]
                for half in range(ppb):
                    page = pt_ref[seq, blk * ppb + half]
                    out.append(pltpu.make_async_copy(
                        v_hbm.at[page, h], vbuf.at[h % slots, k, pl.ds(half * PAGE_SIZE, PAGE_SIZE), :],
                        sems.at[h % slots]))
        return out

    @pl.when(b == 0)
    def _():
        for h in range(PV_AHEAD):
            for c in copies(b, h):
                c.start()

    for h in range(N_HEADS):
        ahead = h + PV_AHEAD
        if ahead < N_HEADS:
            for c in copies(b, ahead):
                c.start()
        else:
            @pl.when(b + 1 < n_seq)
            def _():
                for c in copies(b + 1, ahead - N_HEADS):
                    c.start()
        for c in copies(b, h):
            c.wait()
        acc = ol_ref[h]
        for t in range(n_new):
            for r in range(MOBA_TOPK):
                p = jnp.where(row == t, pc_ref[h, :, r * MOBA_BLOCK:(r + 1) * MOBA_BLOCK], 0.0).astype(BF16)
                acc = acc + jnp.dot(p, vbuf[h % slots, t * MOBA_TOPK + r].astype(BF16), preferred_element_type=F32)
        o_ref[h] = acc


def _page_specs(n):
    def spec(r):
        return pl.BlockSpec((None, N_HEADS, PAGE_SIZE, HEAD_DIM),
                            lambda bi, g, pt: (pt[bi, g * n + r], 0, 0, 0))
    return [spec(r) for r in range(n)]


def _moba_sample(q, k_new, v_new, cache_k, cache_v, page_table):
    db, h, t, d = q.shape
    n_pages = page_table.shape[1]
    assert (n_pages * PAGE_SIZE) % MOBA_BLOCK == 0, "cached keys must end on a MoBA block boundary"
    assert n_pages % PAGES_PER_STEP == 0 and t <= SAMPLE_Q_ROWS and h % 2 == 0
    n_full = n_pages * PAGE_SIZE // MOBA_BLOCK
    n_keys = n_pages * PAGE_SIZE
    rows = SAMPLE_Q_ROWS
    steps = n_pages // PAGES_PER_STEP
    pad = lambda a: jnp.pad(a, ((0, 0), (0, 0), (0, rows - t), (0, 0)))
    qp, knp, vnp = pad(q), pad(k_new), pad(v_new)
    per_seq = pl.BlockSpec((None, h, rows, d), lambda bi, g, pt: (bi, 0, 0, 0))
    keys_spec = pl.BlockSpec((None, h, rows, PAGES_PER_STEP * PAGE_SIZE), lambda bi, g, pt: (bi, 0, 0, g))

    scores = pl.pallas_call(
        _sample_scores_kernel,
        out_shape=jax.ShapeDtypeStruct((db, h, rows, n_keys), F32),
        grid_spec=pltpu.PrefetchScalarGridSpec(
            num_scalar_prefetch=1, grid=(db, steps),
            in_specs=[per_seq] + _page_specs(PAGES_PER_STEP), out_specs=keys_spec),
        compiler_params=_params("parallel", "arbitrary"),
        name="moba_sample_scores",
    )(page_table, qp, *([cache_k] * PAGES_PER_STEP))

    seq4 = lambda last: pl.BlockSpec((None, h, rows, last), lambda bi: (bi, 0, 0, 0))
    probs, sel, o_new = pl.pallas_call(
        functools.partial(_sample_select_kernel, n_new=t, n_full=n_full),
        out_shape=(jax.ShapeDtypeStruct((db, h, rows, MOBA_TOPK * MOBA_BLOCK), F32),
                   jax.ShapeDtypeStruct((db, h, rows, LANES), jnp.int32),
                   jax.ShapeDtypeStruct((db, h, rows, d), F32)),
        grid=(db,),
        in_specs=[seq4(d), seq4(d), seq4(d), seq4(n_keys)],
        out_specs=(seq4(MOBA_TOPK * MOBA_BLOCK), seq4(LANES), seq4(d)),
        compiler_params=_params("parallel"),
        name="moba_sample_select",
    )(qp, knp, vnp, scores)
    sel = sel[:, :, :t, :MOBA_TOPK].reshape(db, h * t * MOBA_TOPK)

    seq2 = lambda last: pl.BlockSpec((None, h, rows, last), lambda bi, pt, sl: (bi, 0, 0, 0))
    out = pl.pallas_call(
        functools.partial(_sample_pv_kernel, n_new=t),
        out_shape=jax.ShapeDtypeStruct((db, h, rows, d), F32),
        grid_spec=pltpu.PrefetchScalarGridSpec(
            num_scalar_prefetch=2, grid=(db,),
            in_specs=[seq2(MOBA_TOPK * MOBA_BLOCK), seq2(d), pl.BlockSpec(memory_space=pl.ANY)],
            out_specs=seq2(d),
            scratch_shapes=[pltpu.VMEM((PV_SLOTS, t * MOBA_TOPK, MOBA_BLOCK, d), F32),
                            pltpu.SemaphoreType.DMA((PV_SLOTS,))]),
        compiler_params=_params("arbitrary"),
        name="moba_sample_pv",
    )(page_table, sel, probs, o_new, cache_v)
    return jnp.transpose(out[:, :, :t], (0, 2, 1, 3)).reshape(db, t, h * d).astype(BF16)


def _proj_ffn_kernel(a_ref, x_ref, wp_ref, g_ref, wg_ref, wu_ref, wd_ref, o_ref, *, chunks, a_transposed):
    a = a_ref[...].T if a_transposed else a_ref[...]
    hid = x_ref[...] + jnp.dot(a, wp_ref[...], preferred_element_type=F32)
    n = _rms(hid, g_ref[...]).astype(BF16)
    width = wg_ref.shape[1] // chunks
    acc = hid
    for c in range(chunks):
        cols = slice(c * width, (c + 1) * width)
        gate = jnp.dot(n, wg_ref[:, cols], preferred_element_type=F32)
        up = jnp.dot(n, wu_ref[:, cols], preferred_element_type=F32)
        act = (_silu(gate) * up).astype(BF16)
        acc = acc + jnp.dot(act, wd_ref[cols, :], preferred_element_type=F32)
    o_ref[...] = acc


def _proj_ffn(a_bf, x, wp_bf, g, wg_bf, wu_bf, wd_bf, layer, tm):
    m, d = x.shape
    hidden = wg_bf.shape[2]
    chunks = hidden // FFN_SLAB if tm >= FFN_SLAB_MIN_ROWS and hidden % FFN_SLAB == 0 else 2
    stacked = lambda w: pl.BlockSpec((None,) + w.shape[1:], lambda i: (layer, 0, 0), pipeline_mode=pl.Buffered(1))
    a_transposed = a_bf.ndim == 3
    if a_transposed:
        nb, ka, per = a_bf.shape
        assert nb * per == m and per % tm == 0
        a_spec = pl.BlockSpec((None, ka, tm), lambda i: (i // (per // tm), 0, i % (per // tm)))
    else:
        ka = a_bf.shape[1]
        a_spec = pl.BlockSpec((tm, ka), lambda i: (i, 0))
    const = lambda shape: pl.BlockSpec(shape, lambda i: (0, 0), pipeline_mode=pl.Buffered(1))
    return pl.pallas_call(
        functools.partial(_proj_ffn_kernel, chunks=chunks, a_transposed=a_transposed),
        out_shape=jax.ShapeDtypeStruct((m, d), F32),
        grid=(m // tm,),
        in_specs=[a_spec, pl.BlockSpec((tm, d), lambda i: (i, 0)),
                  const(wp_bf.shape), const((1, d)), stacked(wg_bf), stacked(wu_bf), stacked(wd_bf)],
        out_specs=pl.BlockSpec((tm, d), lambda i: (i, 0)),
        compiler_params=_params("parallel"),
        name="proj_ffn",
    )(a_bf, x, wp_bf, g[layer].reshape(1, d), wg_bf, wu_bf, wd_bf)


def _inproj_kernel(x_ref, g_ref, wz_ref, wx_ref, wdt_ref, z_ref, xbc_ref, dt_ref):
    n = _rms(x_ref[...], g_ref[...]).astype(BF16)
    z_ref[...] = jnp.dot(n, wz_ref[...], preferred_element_type=F32)
    xbc_ref[...] = jnp.dot(n, wx_ref[...], preferred_element_type=F32)
    dt_ref[...] = jnp.dot(n, wdt_ref[...], preferred_element_type=F32)


def _inproj(x, g, wz_bf, wx_bf, wdt_bf, tm):
    m, d = x.shape
    const = lambda shape: pl.BlockSpec(shape, lambda i: (0, 0), pipeline_mode=pl.Buffered(1))
    tile = lambda n: pl.BlockSpec((tm, n), lambda i: (i, 0))
    return pl.pallas_call(
        _inproj_kernel,
        out_shape=(jax.ShapeDtypeStruct((m, SSM_INNER), F32), jax.ShapeDtypeStruct((m, CONV_DIM), F32),
                   jax.ShapeDtypeStruct((m, LANES), F32)),
        grid=(m // tm,),
        in_specs=[tile(d), const((1, d)), const(wz_bf.shape), const(wx_bf.shape), const(wdt_bf.shape)],
        out_specs=(tile(SSM_INNER), tile(CONV_DIM), tile(LANES)),
        compiler_params=_params("parallel"),
        name="ssm_in_proj",
    )(x, g.reshape(1, d), wz_bf, wx_bf, wdt_bf)


def _ssd_kernel(*refs, rows, has_init):
    if has_init:
        (z_ref, xbc_ref, dt_ref, cw_ref, cb_ref, dtb_ref, alog_ref, dsk_ref, gn_ref, conv0_ref, ssm0_ref,
         y_ref, hout_ref, buf_ref, act_ref, ysc_ref, ht_ref) = refs
    else:
        (z_ref, xbc_ref, dt_ref, cw_ref, cb_ref, dtb_ref, alog_ref, dsk_ref, gn_ref,
         y_ref, hout_ref, buf_ref, act_ref, ysc_ref, ht_ref) = refs
    L = SSD_CHUNK
    P, N, G = SSM_HEADDIM, SSM_STATE, SSM_GROUPS
    PAIRS = SSM_HEADS // 2
    PPG = PAIRS // G
    GW = SSM_HPG * P
    assert L == N == 2 * P == LANES
    c = pl.program_id(1)
    last = pl.num_programs(1) - 1
    tail = SUBLANES - (SSM_CONV - 1)

    @pl.when(c == 0)
    def _():
        buf_ref[...] = jnp.zeros(buf_ref.shape, F32)
        if has_init:
            buf_ref[0:SUBLANES, :] = conv0_ref[...]
            for pair in range(PAIRS):
                two = jnp.concatenate([ssm0_ref[2 * pair], ssm0_ref[2 * pair + 1]], axis=0)
                ht_ref[pair] = two.T
        else:
            ht_ref[...] = jnp.zeros(ht_ref.shape, F32)

    buf_ref[SUBLANES:SUBLANES + rows, :] = xbc_ref[...]
    conv = cb_ref[...] + buf_ref[tail:tail + L, :] * cw_ref[0:1, :]
    for w in range(1, SSM_CONV):
        conv = conv + buf_ref[tail + w:tail + w + L, :] * cw_ref[w:w + 1, :]
    act_ref[...] = _silu(conv)
    if rows == L:
        buf_ref[0:SUBLANES, :] = buf_ref[L:L + SUBLANES, :]

    x = dt_ref[...] + dtb_ref[...]
    dt = jnp.maximum(x, 0.0) + jnp.log1p(jnp.exp(-jnp.abs(x)))
    if rows < L:
        dt = jnp.where(lax.broadcasted_iota(jnp.int32, dt.shape, 0) < rows, dt, 0.0)
    a = dt * (-jnp.exp(alog_ref[...]))
    ti = lax.broadcasted_iota(jnp.int32, (L, L), 0)
    si = lax.broadcasted_iota(jnp.int32, (L, L), 1)
    causal = ti >= si
    a_cs = jnp.dot(jnp.where(causal, 1.0, 0.0), a, precision=lax.Precision.HIGHEST,
                   preferred_element_type=F32)
    a_tot = a_cs[L - 1:L, :]
    ea_tot = jnp.exp(a_tot)
    a_cs_t = a_cs.T
    dt_t = dt.T
    dtde_t = (dt * jnp.exp(a_tot - a_cs)).T
    first = lax.broadcasted_iota(jnp.int32, (L, LANES), 1) < P

    for g in range(G):
        bg = act_ref[:, SSM_INNER + g * N:SSM_INNER + (g + 1) * N]
        cg = act_ref[:, SSM_INNER + G * N + g * N:SSM_INNER + G * N + (g + 1) * N]
        cb = lax.dot_general(cg.astype(BF16), bg.astype(BF16), _NT, preferred_element_type=F32)
        bg_t = bg.T
        for pr in range(PPG):
            pair = g * PPG + pr
            lanes = slice(pair * LANES, (pair + 1) * LANES)
            xs = act_ref[:, lanes]
            ht = ht_ref[pair]
            y = dsk_ref[:, lanes] * xs
            upd = jnp.zeros((N, LANES), F32)
            for second in range(2):
                e = 2 * pair + second
                mine = lambda v: (jnp.where(first, 0.0, v) if second else jnp.where(first, v, 0.0)).astype(BF16)
                xs_e, ht_e = mine(xs), mine(ht)
                acs = jnp.broadcast_to(a_cs[:, e:e + 1], (L, L))
                decay = jnp.exp(jnp.where(causal, acs - a_cs_t[e:e + 1, :], -jnp.inf))
                w = (cb * decay * dt_t[e:e + 1, :]).astype(BF16)
                c_in = (cg * jnp.exp(acs)).astype(BF16)
                y = (y + jnp.dot(w, xs_e, preferred_element_type=F32)
                     + jnp.dot(c_in, ht_e, preferred_element_type=F32))
                b_out = (bg_t * dtde_t[e:e + 1, :]).astype(BF16)
                upd = upd + jnp.dot(b_out, xs_e, preferred_element_type=F32)
            ysc_ref[:, lanes] = y
            keep = jnp.where(first[0:1, :], ea_tot[:, 2 * pair:2 * pair + 1], ea_tot[:, 2 * pair + 1:2 * pair + 2])
            ht_ref[pair] = keep * ht + upd

    z_in = z_ref[...]
    for g in range(G):
        lanes = slice(g * GW, (g + 1) * GW)
        yg = ysc_ref[0:rows, lanes] * _silu(z_in[:, lanes])
        y_ref[:, lanes] = _rms(yg, gn_ref[:, lanes]).astype(y_ref.dtype)

    @pl.when(c == last)
    def _():
        for pair in range(PAIRS):
            two = ht_ref[pair].T
            hout_ref[2 * pair] = two[0:P]
            hout_ref[2 * pair + 1] = two[P:2 * P]


def _ssd(z, xbc, dt, conv_w, conv_b, dt_bias, a_log, d_skip, gate_norm, conv0=None, ssm0=None):
    b, t, _ = z.shape
    has_init = conv0 is not None
    rows = min(SSD_CHUNK, t)
    assert t % rows == 0 and (rows == SSD_CHUNK or t == rows)
    nc = t // rows
    if rows < SSD_CHUNK:
        dt = jnp.pad(dt, ((0, 0), (0, SSD_CHUNK - rows), (0, 0)))
    lane_pad = lambda v: jnp.pad(v.astype(F32), (0, LANES - v.shape[0])).reshape(1, LANES)
    tile = lambda n: pl.BlockSpec((None, rows, n), lambda bi, ci: (bi, ci, 0))
    const = lambda shape: pl.BlockSpec(shape, lambda bi, ci: (0,) * len(shape))
    state_spec = pl.BlockSpec((None, SSM_HEADS, SSM_HEADDIM, SSM_STATE), lambda bi, ci: (bi, 0, 0, 0))
    in_specs = [tile(SSM_INNER), tile(CONV_DIM), pl.BlockSpec((None, SSD_CHUNK, LANES), lambda bi, ci: (bi, ci, 0)),
                const((SSM_CONV, CONV_DIM)), const((1, CONV_DIM)),
                const((1, LANES)), const((1, LANES)), const((1, SSM_INNER)), const((1, SSM_INNER))]
    args = [z, xbc, dt, conv_w, conv_b.reshape(1, CONV_DIM), lane_pad(dt_bias), lane_pad(a_log),
            jnp.repeat(d_skip.astype(F32), SSM_HEADDIM).reshape(1, SSM_INNER), gate_norm.reshape(1, SSM_INNER)]
    if has_init:
        in_specs += [pl.BlockSpec((None, SUBLANES, CONV_DIM), lambda bi, ci: (bi, 0, 0)), state_spec]
        args += [jnp.pad(conv0, ((0, 0), (SUBLANES - (SSM_CONV - 1), 0), (0, 0))), ssm0]
    scratch = [pltpu.VMEM((SUBLANES + SSD_CHUNK, CONV_DIM), F32),
               pltpu.VMEM((SSD_CHUNK, CONV_DIM), F32),
               pltpu.VMEM((SSD_CHUNK, SSM_INNER), F32),
               pltpu.VMEM((SSM_HEADS // 2, SSM_STATE, 2 * SSM_HEADDIM), F32)]
    y_dtype = BF16 if rows % 16 == 0 else F32
    return pl.pallas_call(
        functools.partial(_ssd_kernel, rows=rows, has_init=has_init),
        out_shape=(jax.ShapeDtypeStruct((b, t, SSM_INNER), y_dtype),
                   jax.ShapeDtypeStruct((b, SSM_HEADS, SSM_HEADDIM, SSM_STATE), F32)),
        grid=(b, nc),
        in_specs=in_specs,
        out_specs=(tile(SSM_INNER), state_spec),
        scratch_shapes=scratch,
        compiler_params=_params("parallel", "arbitrary"),
        name="ssd_chunk",
    )(*args)


def kernel(x_prompt, x_sample, cache_k, cache_v, page_table, state_conv, state_ssm, attn_norm, w_qkv, q_norm, k_norm, w_o, ssm_norm, w_in, conv_w, conv_b, dt_bias, a_log, d_skip, gate_norm, w_out, ffn_norm, w_gate, w_up, w_down):
    b, s, d = x_prompt.shape
    db, t, _ = x_sample.shape
    past = page_table.shape[1] * PAGE_SIZE
    tm_p, tm_s = 512, db * t
    bf = lambda w: w.astype(BF16)

    w_qkv_bf, w_o_bf = bf(w_qkv[0]), bf(w_o[0])
    ffn = (ffn_norm, bf(w_gate), bf(w_up), bf(w_down))
    ffn0, ffn1 = ffn + (0,), ffn + (1,)
    q_p, k_p, v_p = _qkv(x_prompt, attn_norm[0], w_qkv_bf, q_norm[0], k_norm[0], jnp.arange(s), tm_p)
    attn_p = _moba_prompt(q_p, k_p, v_p)
    pos_s = past + jnp.tile(jnp.arange(t), db)
    xs_rows = x_sample.reshape(1, db * t, d)
    qkv_s = _qkv(xs_rows, attn_norm[0], w_qkv_bf, q_norm[0], k_norm[0], pos_s, tm_s)
    q_s, k_s, v_s = (jnp.transpose(a.reshape(N_HEADS, db, t, HEAD_DIM), (1, 0, 2, 3)) for a in qkv_s)
    attn_s = _moba_sample(q_s, k_s, v_s, cache_k[0], cache_v[0], page_table)
    h_p = _proj_ffn(attn_p, x_prompt.reshape(b * s, d), w_o_bf, *ffn0, tm_p)
    h_s = _proj_ffn(attn_s.reshape(db * t, d), x_sample.reshape(db * t, d), w_o_bf, *ffn0, tm_s)

    w_in1 = w_in[0]
    wz_bf = bf(w_in1[:, :SSM_INNER])
    wx_bf = bf(w_in1[:, SSM_INNER:SSM_INNER + CONV_DIM])
    wdt_bf = bf(jnp.pad(w_in1[:, SSM_INNER + CONV_DIM:], ((0, 0), (0, LANES - SSM_HEADS))))
    w_out_bf = bf(w_out[0])
    conv = (conv_w[0], conv_b[0])
    ssd_w = (dt_bias[0], a_log[0], d_skip[0], gate_norm[0])
    z_p, xbc_p, dt_p = _inproj(h_p, ssm_norm[0], wz_bf, wx_bf, wdt_bf, tm_p)
    z_s, xbc_s, dt_s = _inproj(h_s, ssm_norm[0], wz_bf, wx_bf, wdt_bf, tm_s)
    xbc_p = xbc_p.reshape(b, s, CONV_DIM)
    xbc_s = xbc_s.reshape(db, t, CONV_DIM)
    y_p, ssm_p = _ssd(z_p.reshape(b, s, SSM_INNER), xbc_p, dt_p.reshape(b, s, LANES), *conv, *ssd_w)
    y_s, ssm_s = _ssd(z_s.reshape(db, t, SSM_INNER), xbc_s, dt_s.reshape(db, t, LANES), *conv, *ssd_w,
                      conv0=state_conv[0], ssm0=state_ssm[0])
    out_p = _proj_ffn(y_p.reshape(b * s, SSM_INNER), h_p, w_out_bf, *ffn1, tm_p)
    out_s = _proj_ffn(bf(y_s).reshape(db * t, SSM_INNER), h_s, w_out_bf, *ffn1, tm_s)

    keep = SSM_CONV - 1
    conv_p = xbc_p[:, s - keep:]
    conv_s = jnp.concatenate([state_conv[0], xbc_s], axis=1)[:, t:]
    return (out_p.reshape(b, s, d), out_s.reshape(db, t, d), k_p[None], v_p[None], k_s[None], v_s[None],
            conv_p[None], ssm_p[None], conv_s[None], ssm_s[None])
```

```python
import functools

import jax
import jax.numpy as jnp
from jax import lax
from jax.experimental import pallas as pl
from jax.experimental.pallas import tpu as pltpu

F32 = jnp.float32
BF16 = jnp.bfloat16

EPS = 1e-6
N_HEADS = 8
HEAD_DIM = 128
ROT_DIM = 32
ROPE_THETA = 500000.0
MOBA_BLOCK = 256
MOBA_TOPK = 3
PAGE_SIZE = 128
SSM_INNER = 2048
SSM_HEADDIM = 64
SSM_HEADS = 32
SSM_GROUPS = 4
SSM_HPG = SSM_HEADS // SSM_GROUPS
SSM_STATE = 128
SSM_CONV = 4
CONV_DIM = SSM_INNER + 2 * SSM_GROUPS * SSM_STATE
SSD_CHUNK = 128

LANES = 128
SUBLANES = 8
VMEM_LIMIT_BYTES = 56 * 2**20
MOBA_UNROLL = 2
MOBA_HEADS_PER_STEP = 4
PAGES_PER_STEP = 32
SSD_CONV_SLAB = 512
PV_SLOTS = 4
PV_AHEAD = 3
FFN_SLAB = 256
FFN_SLAB_MIN_ROWS = 512
SAMPLE_Q_ROWS = 8

LOG2_E = 1.4426950408889634

_NT = (((1,), (1,)), ((), ()))


def _params(*semantics):
    return pltpu.CompilerParams(dimension_semantics=semantics, vmem_limit_bytes=VMEM_LIMIT_BYTES)


def _rms(x, g):
    return x * lax.rsqrt(jnp.mean(x * x, axis=-1, keepdims=True) + EPS) * g


def _silu(x):
    return x / (1.0 + jnp.exp(-x))


def _qkv_kernel(x_ref, g_ref, w_ref, qg_ref, kg_ref, cos_ref, sa_ref, sb_ref, q_ref, k_ref, v_ref):
    n = _rms(x_ref[...], g_ref[...]).astype(BF16)
    cos, sa, sb = cos_ref[...], sa_ref[...], sb_ref[...]
    d = N_HEADS * HEAD_DIM
    for part, (o_ref, gain_ref) in enumerate(((q_ref, qg_ref), (k_ref, kg_ref), (v_ref, None))):
        y = jnp.dot(n, w_ref[:, part * d:(part + 1) * d], preferred_element_type=F32)
        for h in range(N_HEADS):
            yh = y[:, h * HEAD_DIM:(h + 1) * HEAD_DIM]
            if gain_ref is not None:
                yh = _rms(yh, gain_ref[...])
                yh = (yh * cos + pltpu.roll(yh, HEAD_DIM - ROT_DIM // 2, 1) * sa
                      + pltpu.roll(yh, ROT_DIM // 2, 1) * sb)
            o_ref[h] = yh


def _rope_tables(pos):
    half = ROT_DIM // 2
    inv = ROPE_THETA ** (-jnp.arange(half, dtype=F32) * 2.0 / ROT_DIM)
    ang = pos.astype(F32)[:, None] * inv[None, :]
    cos, sin = jnp.cos(ang), jnp.sin(ang)
    t = pos.shape[0]
    cos_t = jnp.concatenate([cos, cos, jnp.ones((t, HEAD_DIM - ROT_DIM), F32)], axis=1)
    sa = jnp.concatenate([-sin, jnp.zeros((t, HEAD_DIM - half), F32)], axis=1)
    sb = jnp.concatenate([jnp.zeros((t, half), F32), sin, jnp.zeros((t, HEAD_DIM - ROT_DIM), F32)], axis=1)
    return cos_t, sa, sb


def _qkv(x, g, w_bf, qg, kg, pos, tm):
    b, t, d = x.shape
    cos_t, sa, sb = _rope_tables(pos)
    row = lambda v: v.reshape(1, -1)
    out = jax.ShapeDtypeStruct((b, N_HEADS, t, HEAD_DIM), F32)
    tab_spec = pl.BlockSpec((tm, HEAD_DIM), lambda bi, i: (i, 0))
    out_spec = pl.BlockSpec((None, N_HEADS, tm, HEAD_DIM), lambda bi, i: (bi, 0, i, 0))
    const = lambda shape: pl.BlockSpec(shape, lambda bi, i: (0,) * len(shape))
    return pl.pallas_call(
        _qkv_kernel,
        out_shape=(out, out, out),
        grid=(b, t // tm),
        in_specs=[pl.BlockSpec((None, tm, d), lambda bi, i: (bi, i, 0)),
                  const((1, d)), const(w_bf.shape), const((1, HEAD_DIM)), const((1, HEAD_DIM)),
                  tab_spec, tab_spec, tab_spec],
        out_specs=(out_spec, out_spec, out_spec),
        compiler_params=_params("parallel", "parallel"),
        name="qkv_norm_rope",
    )(x, row(g), w_bf, row(qg), row(kg), cos_t, sa, sb)


def _topk_rank(entries, gate, index):
    rank = jnp.zeros(gate.shape, F32)
    for j, c in enumerate(entries):
        ahead = (c > gate) | ((c == gate) & (j < index))
        rank = rank + jnp.where(ahead, 1.0, 0.0)
    return rank


def _moba_prompt_kernel(q_ref, k_ref, v_ref, o_ref, kb_ref, vt_ref, km_ref, bias_ref):
    i = pl.program_id(2)
    heads = q_ref.shape[0]
    nb = k_ref.shape[1] // MOBA_BLOCK
    scale = HEAD_DIM ** -0.5

    @pl.when(i == 0)
    def _():
        kb_ref[...] = k_ref[...].astype(BF16)
        km_ref[...] = jnp.zeros(km_ref.shape, F32)
        for h in range(heads):
            for j in range(nb):
                rows = slice(j * MOBA_BLOCK, (j + 1) * MOBA_BLOCK)
                vt_ref[h, j] = v_ref[h, rows, :].T.astype(BF16)
                km_ref[h, j:j + 1, :] = jnp.mean(k_ref[h, rows, :], axis=0, keepdims=True)

    qbs = []
    for h in range(heads):
        qf = q_ref[h]
        qbs.append((qf * (scale * LOG2_E)).astype(BF16))
        gate = lax.dot_general(km_ref[h], qf, _NT, precision=lax.Precision.HIGHEST,
                               preferred_element_type=F32)
        blk = lax.broadcasted_iota(jnp.int32, gate.shape, 0)
        past = blk < i
        gate = jnp.where(past, gate, -jnp.inf)
        rank = _topk_rank([gate[j:j + 1, :] for j in range(nb)], gate, blk)
        bias_ref[h] = jnp.where((rank < MOBA_TOPK) & past, 0.0, -jnp.inf)

    def scores(h, j):
        start = j * MOBA_BLOCK
        if not isinstance(j, int):
            start = pl.multiple_of(start, MOBA_BLOCK)
        kj = kb_ref[h, pl.ds(start, MOBA_BLOCK), :]
        return lax.dot_general(kj, qbs[h], _NT, preferred_element_type=F32)

    def attend(n_lead):
        for h in range(heads):
            s = scores(h, i)
            ki = lax.broadcasted_iota(jnp.int32, s.shape, 0)
            qi = lax.broadcasted_iota(jnp.int32, s.shape, 1)
            s_own = jnp.where(ki <= qi, s, -jnp.inf)
            lead = [(j, scores(h, j), bias_ref[h, j:j + 1, :]) for j in range(n_lead)]
            m = jnp.max(s_own, axis=0, keepdims=True)
            for _, s, bias in lead:
                m = jnp.maximum(m, jnp.max(s, axis=0, keepdims=True) + bias)
            blocks = [(i, s_own, m)] + [(j, s, m - bias) for j, s, bias in lead]
            l, acc = 0.0, 0.0
            for j, s, shift in blocks:
                p = jnp.exp2(s - shift)
                l = l + jnp.sum(p, axis=0, keepdims=True)
                acc = acc + jnp.dot(vt_ref[h, j], p.astype(BF16), preferred_element_type=F32)
            o_ref[h * HEAD_DIM:(h + 1) * HEAD_DIM, :] = (acc / l).astype(o_ref.dtype)

    groups = pl.cdiv(i, MOBA_UNROLL)
    for v in range(nb // MOBA_UNROLL + 1):
        pl.when(groups == v)(functools.partial(attend, v * MOBA_UNROLL))


def _moba_prompt(q, k, v):
    b, h, s, d = q.shape
    hs = MOBA_HEADS_PER_STEP
    assert s % (MOBA_BLOCK * MOBA_UNROLL) == 0 and h % hs == 0
    tq = MOBA_BLOCK
    nb = s // MOBA_BLOCK
    nb_pad = pl.cdiv(nb, SUBLANES) * SUBLANES
    full = pl.BlockSpec((None, hs, s, d), lambda bi, hi, i: (bi, hi, 0, 0))
    return pl.pallas_call(
        _moba_prompt_kernel,
        out_shape=jax.ShapeDtypeStruct((b, h * d, s), BF16),
        grid=(b, h // hs, s // tq),
        in_specs=[pl.BlockSpec((None, hs, tq, d), lambda bi, hi, i: (bi, hi, i, 0)), full, full],
        out_specs=pl.BlockSpec((None, hs * d, tq), lambda bi, hi, i: (bi, hi, i)),
        scratch_shapes=[pltpu.VMEM((hs, s, d), BF16),
                        pltpu.VMEM((hs, nb, d, MOBA_BLOCK), BF16),
                        pltpu.VMEM((hs, nb_pad, d), F32),
                        pltpu.VMEM((hs, nb_pad, tq), F32)],
        compiler_params=_params("parallel", "parallel", "arbitrary"),
        name="moba_prompt",
    )(q, k, v)


def _sample_scores_kernel(pt_ref, q_ref, *refs):
    del pt_ref
    k_refs, s_ref = refs[:-1], refs[-1]
    for r, k_ref in enumerate(k_refs):
        for h in range(N_HEADS):
            s = lax.dot_general(q_ref[h].astype(BF16), k_ref[h].astype(BF16), _NT, preferred_element_type=F32)
            s_ref[h, :, r * PAGE_SIZE:(r + 1) * PAGE_SIZE] = s


def _sample_select_kernel(q_ref, kn_ref, vn_ref, s_ref, pc_ref, sel_ref, ol_ref, *, n_new, n_full):
    scale = HEAD_DIM ** -0.5
    rows = s_ref.shape[1]
    n_top = min(MOBA_TOPK, n_full)
    lane = lax.broadcasted_iota(jnp.int32, (rows, LANES), 1)
    lane_f = lane.astype(F32)
    row = lax.broadcasted_iota(jnp.int32, (rows, 1), 0)
    for h in range(N_HEADS):
        s = s_ref[h]
        blocks = [s[:, j * MOBA_BLOCK:(j + 1) * MOBA_BLOCK] for j in range(n_full)]
        cols = [jnp.sum(blk, axis=1, keepdims=True) * (1.0 / MOBA_BLOCK) for blk in blocks]
        gate = jnp.full((rows, LANES), -jnp.inf, F32)
        for j, c in enumerate(cols):
            gate = jnp.where(lane == j, c, gate)
        rank = _topk_rank(cols, gate, lane)
        picks = [jnp.sum(jnp.where((rank == r) & (lane < n_full), lane_f, 0.0), axis=1, keepdims=True)
                 for r in range(n_top)]
        s_sel = []
        for pick in picks:
            g = jnp.zeros((rows, MOBA_BLOCK), F32)
            for j, blk in enumerate(blocks):
                g = g + jnp.where(pick == j, blk, 0.0)
            s_sel.append(g * scale)
        q = q_ref[h]
        s_new = []
        for u in range(n_new):
            su = jnp.sum(q * kn_ref[h, u:u + 1, :], axis=1, keepdims=True) * scale
            s_new.append(jnp.where(row >= u, su, -jnp.inf))
        m = functools.reduce(jnp.maximum, [jnp.max(g, axis=1, keepdims=True) for g in s_sel] + s_new)
        p_sel = [jnp.exp(g - m) for g in s_sel]
        p_new = [jnp.exp(su - m) for su in s_new]
        denom = functools.reduce(jnp.add, [jnp.sum(p, axis=1, keepdims=True) for p in p_sel] + p_new)
        ids = jnp.zeros((rows, LANES), F32)
        for r in range(MOBA_TOPK):
            cols_r = slice(r * MOBA_BLOCK, (r + 1) * MOBA_BLOCK)
            if r < n_top:
                pc_ref[h, :, cols_r] = p_sel[r] / denom
                ids = jnp.where(lane == r, picks[r], ids)
            else:
                pc_ref[h, :, cols_r] = jnp.zeros((rows, MOBA_BLOCK), F32)
        sel_ref[h] = ids.astype(jnp.int32)
        o = jnp.zeros((rows, HEAD_DIM), F32)
        for u, pu in enumerate(p_new):
            o = o + (pu / denom) * vn_ref[h, u:u + 1, :]
        ol_ref[h] = o


def _sample_pv_kernel(pt_ref, sel_ref, pc_ref, ol_ref, v_hbm, o_ref, vbuf, sems, *, n_new):
    b = pl.program_id(0)
    n_seq = pl.num_programs(0)
    rows = pc_ref.shape[1]
    ppb = MOBA_BLOCK // PAGE_SIZE
    slots = vbuf.shape[0]
    assert N_HEADS % slots == 0 and PV_AHEAD < slots <= N_HEADS
    row = lax.broadcasted_iota(jnp.int32, (rows, MOBA_BLOCK), 0)

    def copies(seq, h):
        out = []
        for t in range(n_new):
            for r in range(MOBA_TOPK):
                k = t * MOBA_TOPK + r
                blk = sel_ref[seq, (h * n_new + t) * MOBA_TOPK + r]
                for half in range(ppb):
                    page = pt_ref[seq, blk * ppb + half]
                    out.append(pltpu.make_async_copy(
                        v_hbm.at[page, h], vbuf.at[h % slots, k, pl.ds(half * PAGE_SIZE, PAGE_SIZE), :],
                        sems.at[h % slots]))
        return out

    @pl.when(b == 0)
    def _():
        for h in range(PV_AHEAD):
            for c in copies(b, h):
                c.start()

    for h in range(N_HEADS):
        ahead = h + PV_AHEAD
        if ahead < N_HEADS:
            for c in copies(b, ahead):
                c.start()
        else:
            @pl.when(b + 1 < n_seq)
            def _():
                for c in copies(b + 1, ahead - N_HEADS):
                    c.start()
        for c in copies(b, h):
            c.wait()
        acc = ol_ref[h]
        for t in range(n_new):
            for r in range(MOBA_TOPK):
                p = jnp.where(row == t, pc_ref[h, :, r * MOBA_BLOCK:(r + 1) * MOBA_BLOCK], 0.0).astype(BF16)
                acc = acc + jnp.dot(p, vbuf[h % slots, t * MOBA_TOPK + r].astype(BF16), preferred_element_type=F32)
        o_ref[h] = acc


def _page_specs(n):
    def spec(r):
        return pl.BlockSpec((None, N_HEADS, PAGE_SIZE, HEAD_DIM),
                            lambda bi, g, pt: (pt[bi, g * n + r], 0, 0, 0))
    return [spec(r) for r in range(n)]


def _moba_sample(q, k_new, v_new, cache_k, cache_v, page_table):
    db, h, t, d = q.shape
    n_pages = page_table.shape[1]
    assert (n_pages * PAGE_SIZE) % MOBA_BLOCK == 0, "cached keys must end on a MoBA block boundary"
    assert n_pages % PAGES_PER_STEP == 0 and t <= SAMPLE_Q_ROWS and h % 2 == 0
    n_full = n_pages * PAGE_SIZE // MOBA_BLOCK
    n_keys = n_pages * PAGE_SIZE
    rows = SAMPLE_Q_ROWS
    steps = n_pages // PAGES_PER_STEP
    pad = lambda a: jnp.pad(a, ((0, 0), (0, 0), (0, rows - t), (0, 0)))
    qp, knp, vnp = pad(q), pad(k_new), pad(v_new)
    per_seq = pl.BlockSpec((None, h, rows, d), lambda bi, g, pt: (bi, 0, 0, 0))
    keys_spec = pl.BlockSpec((None, h, rows, PAGES_PER_STEP * PAGE_SIZE), lambda bi, g, pt: (bi, 0, 0, g))

    scores = pl.pallas_call(
        _sample_scores_kernel,
        out_shape=jax.ShapeDtypeStruct((db, h, rows, n_keys), F32),
        grid_spec=pltpu.PrefetchScalarGridSpec(
            num_scalar_prefetch=1, grid=(db, steps),
            in_specs=[per_seq] + _page_specs(PAGES_PER_STEP), out_specs=keys_spec),
        compiler_params=_params("parallel", "arbitrary"),
        name="moba_sample_scores",
    )(page_table, qp, *([cache_k] * PAGES_PER_STEP))

    seq4 = lambda last: pl.BlockSpec((None, h, rows, last), lambda bi: (bi, 0, 0, 0))
    probs, sel, o_new = pl.pallas_call(
        functools.partial(_sample_select_kernel, n_new=t, n_full=n_full),
        out_shape=(jax.ShapeDtypeStruct((db, h, rows, MOBA_TOPK * MOBA_BLOCK), F32),
                   jax.ShapeDtypeStruct((db, h, rows, LANES), jnp.int32),
                   jax.ShapeDtypeStruct((db, h, rows, d), F32)),
        grid=(db,),
        in_specs=[seq4(d), seq4(d), seq4(d), seq4(n_keys)],
        out_specs=(seq4(MOBA_TOPK * MOBA_BLOCK), seq4(LANES), seq4(d)),
        compiler_params=_params("parallel"),
        name="moba_sample_select",
    )(qp, knp, vnp, scores)
    sel = sel[:, :, :t, :MOBA_TOPK].reshape(db, h * t * MOBA_TOPK)

    seq2 = lambda last: pl.BlockSpec((None, h, rows, last), lambda bi, pt, sl: (bi, 0, 0, 0))
    out = pl.pallas_call(
        functools.partial(_sample_pv_kernel, n_new=t),
        out_shape=jax.ShapeDtypeStruct((db, h, rows, d), F32),
        grid_spec=pltpu.PrefetchScalarGridSpec(
            num_scalar_prefetch=2, grid=(db,),
            in_specs=[seq2(MOBA_TOPK * MOBA_BLOCK), seq2(d), pl.BlockSpec(memory_space=pl.ANY)],
            out_specs=seq2(d),
            scratch_shapes=[pltpu.VMEM((PV_SLOTS, t * MOBA_TOPK, MOBA_BLOCK, d), F32),
                            pltpu.SemaphoreType.DMA((PV_SLOTS,))]),
        compiler_params=_params("arbitrary"),
        name="moba_sample_pv",
    )(page_table, sel, probs, o_new, cache_v)
    return jnp.transpose(out[:, :, :t], (0, 2, 1, 3)).reshape(db, t, h * d).astype(BF16)


def _proj_ffn_kernel(a_ref, x_ref, wp_ref, g_ref, wg_ref, wu_ref, wd_ref, o_ref, *, chunks, a_transposed):
    a = a_ref[...].T if a_transposed else a_ref[...]
    hid = x_ref[...] + jnp.dot(a, wp_ref[...], preferred_element_type=F32)
    n = _rms(hid, g_ref[...]).astype(BF16)
    width = wg_ref.shape[1] // chunks
    acc = hid
    for c in range(chunks):
        cols = slice(c * width, (c + 1) * width)
        gate = jnp.dot(n, wg_ref[:, cols], preferred_element_type=F32)
        up = jnp.dot(n, wu_ref[:, cols], preferred_element_type=F32)
        act = (_silu(gate) * up).astype(BF16)
        acc = acc + jnp.dot(act, wd_ref[cols, :], preferred_element_type=F32)
    o_ref[...] = acc


def _proj_ffn(a_bf, x, wp_bf, g, wg_bf, wu_bf, wd_bf, layer, tm):
    m, d = x.shape
    hidden = wg_bf.shape[2]
    chunks = hidden // FFN_SLAB if tm >= FFN_SLAB_MIN_ROWS and hidden % FFN_SLAB == 0 else 2
    stacked = lambda w: pl.BlockSpec((None,) + w.shape[1:], lambda i: (layer, 0, 0), pipeline_mode=pl.Buffered(1))
    a_transposed = a_bf.ndim == 3
    if a_transposed:
        nb, ka, per = a_bf.shape
        assert nb * per == m and per % tm == 0
        a_spec = pl.BlockSpec((None, ka, tm), lambda i: (i // (per // tm), 0, i % (per // tm)))
    else:
        ka = a_bf.shape[1]
        a_spec = pl.BlockSpec((tm, ka), lambda i: (i, 0))
    const = lambda shape: pl.BlockSpec(shape, lambda i: (0, 0), pipeline_mode=pl.Buffered(1))
    return pl.pallas_call(
        functools.partial(_proj_ffn_kernel, chunks=chunks, a_transposed=a_transposed),
        out_shape=jax.ShapeDtypeStruct((m, d), F32),
        grid=(m // tm,),
        in_specs=[a_spec, pl.BlockSpec((tm, d), lambda i: (i, 0)),
                  const(wp_bf.shape), const((1, d)), stacked(wg_bf), stacked(wu_bf), stacked(wd_bf)],
        out_specs=pl.BlockSpec((tm, d), lambda i: (i, 0)),
        compiler_params=_params("parallel"),
        name="proj_ffn",
    )(a_bf, x, wp_bf, g[layer].reshape(1, d), wg_bf, wu_bf, wd_bf)


def _inproj_kernel(x_ref, g_ref, wz_ref, wx_ref, wdt_ref, z_ref, xbc_ref, dt_ref):
    n = _rms(x_ref[...], g_ref[...]).astype(BF16)
    z_ref[...] = jnp.dot(n, wz_ref[...], preferred_element_type=F32)
    xbc_ref[...] = jnp.dot(n, wx_ref[...], preferred_element_type=F32)
    dt_ref[...] = jnp.dot(n, wdt_ref[...], preferred_element_type=F32)


def _inproj(x, g, wz_bf, wx_bf, wdt_bf, tm):
    m, d = x.shape
    const = lambda shape: pl.BlockSpec(shape, lambda i: (0, 0), pipeline_mode=pl.Buffered(1))
    tile = lambda n: pl.BlockSpec((tm, n), lambda i: (i, 0))
    return pl.pallas_call(
        _inproj_kernel,
        out_shape=(jax.ShapeDtypeStruct((m, SSM_INNER), F32), jax.ShapeDtypeStruct((m, CONV_DIM), F32),
                   jax.ShapeDtypeStruct((m, LANES), F32)),
        grid=(m // tm,),
        in_specs=[tile(d), const((1, d)), const(wz_bf.shape), const(wx_bf.shape), const(wdt_bf.shape)],
        out_specs=(tile(SSM_INNER), tile(CONV_DIM), tile(LANES)),
        compiler_params=_params("parallel"),
        name="ssm_in_proj",
    )(x, g.reshape(1, d), wz_bf, wx_bf, wdt_bf)


def _ssd_kernel(*refs, rows, has_init):
    if has_init:
        (z_ref, xbc_ref, dt_ref, cw_ref, cb_ref, dtb_ref, alog_ref, dsk_ref, gn_ref, conv0_ref, ssm0_ref,
         y_ref, hout_ref, buf_ref, act_ref, ysc_ref, ht_ref) = refs
    else:
        (z_ref, xbc_ref, dt_ref, cw_ref, cb_ref, dtb_ref, alog_ref, dsk_ref, gn_ref,
         y_ref, hout_ref, buf_ref, act_ref, ysc_ref, ht_ref) = refs
    L = SSD_CHUNK
    P, N, G = SSM_HEADDIM, SSM_STATE, SSM_GROUPS
    PAIRS = SSM_HEADS // 2
    PPG = PAIRS // G
    GW = SSM_HPG * P
    assert L == N == 2 * P == LANES
    c = pl.program_id(1)
    last = pl.num_programs(1) - 1
    tail = SUBLANES - (SSM_CONV - 1)

    @pl.when(c == 0)
    def _():
        buf_ref[...] = jnp.zeros(buf_ref.shape, F32)
        if has_init:
            buf_ref[0:SUBLANES, :] = conv0_ref[...]
            for pair in range(PAIRS):
                two = jnp.concatenate([ssm0_ref[2 * pair], ssm0_ref[2 * pair + 1]], axis=0)
                ht_ref[pair] = two.T
        else:
            ht_ref[...] = jnp.zeros(ht_ref.shape, F32)

    if rows == L:
        row8 = lax.broadcasted_iota(jnp.int32, (SUBLANES, SSD_CONV_SLAB), 0)
        for c0 in range(0, CONV_DIM, SSD_CONV_SLAB):
            cols = slice(c0, c0 + SSD_CONV_SLAB)
            xs = xbc_ref[:, cols]
            before = buf_ref[0:SUBLANES, cols]
            conv = cb_ref[:, cols] + xs * cw_ref[SSM_CONV - 1:SSM_CONV, cols]
            for k in range(1, SSM_CONV):
                rolled = pltpu.roll(xs, k, 0)
                head = jnp.where(row8 < k, pltpu.roll(before, k, 0), rolled[0:SUBLANES])
                shifted = jnp.concatenate([head, rolled[SUBLANES:]], axis=0)
                conv = conv + shifted * cw_ref[SSM_CONV - 1 - k:SSM_CONV - k, cols]
            act_ref[:, cols] = _silu(conv)
            buf_ref[0:SUBLANES, cols] = xs[L - SUBLANES:L]
    else:
        buf_ref[SUBLANES:SUBLANES + rows, :] = xbc_ref[...]
        conv = cb_ref[...] + buf_ref[tail:tail + L, :] * cw_ref[0:1, :]
        for w in range(1, SSM_CONV):
            conv = conv + buf_ref[tail + w:tail + w + L, :] * cw_ref[w:w + 1, :]
        act_ref[...] = _silu(conv)

    x = dt_ref[...] + dtb_ref[...]
    dt = jnp.maximum(x, 0.0) + jnp.log1p(jnp.exp(-jnp.abs(x)))
    if rows < L:
        dt = jnp.where(lax.broadcasted_iota(jnp.int32, dt.shape, 0) < rows, dt, 0.0)
    a = dt * (-jnp.exp(alog_ref[...]))
    ti = lax.broadcasted_iota(jnp.int32, (L, L), 0)
    si = lax.broadcasted_iota(jnp.int32, (L, L), 1)
    causal = ti >= si
    a_cs = jnp.dot(jnp.where(causal, 1.0, 0.0), a, precision=lax.Precision.HIGHEST,
                   preferred_element_type=F32)
    a_tot = a_cs[L - 1:L, :]
    ea_tot = jnp.exp(a_tot)
    a_cs_t = a_cs.T
    dt_t = dt.T
    dtde_t = (dt * jnp.exp(a_tot - a_cs)).T
    first = lax.broadcasted_iota(jnp.int32, (L, LANES), 1) < P

    for g in range(G):
        bg = act_ref[:, SSM_INNER + g * N:SSM_INNER + (g + 1) * N]
        cg = act_ref[:, SSM_INNER + G * N + g * N:SSM_INNER + G * N + (g + 1) * N]
        cb = lax.dot_general(cg.astype(BF16), bg.astype(BF16), _NT, preferred_element_type=F32)
        bg_t = bg.T
        for pr in range(PPG):
            pair = g * PPG + pr
            lanes = slice(pair * LANES, (pair + 1) * LANES)
            xs = act_ref[:, lanes]
            ht = ht_ref[pair]
            y = dsk_ref[:, lanes] * xs
            upd = jnp.zeros((N, LANES), F32)
            for second in range(2):
                e = 2 * pair + second
                mine = lambda v: (jnp.where(first, 0.0, v) if second else jnp.where(first, v, 0.0)).astype(BF16)
                xs_e, ht_e = mine(xs), mine(ht)
                acs = jnp.broadcast_to(a_cs[:, e:e + 1], (L, L))
                decay = jnp.exp(jnp.where(causal, acs - a_cs_t[e:e + 1, :], -jnp.inf))
                w = (cb * decay * dt_t[e:e + 1, :]).astype(BF16)
                c_in = (cg * jnp.exp(acs)).astype(BF16)
                y = (y + jnp.dot(w, xs_e, preferred_element_type=F32)
                     + jnp.dot(c_in, ht_e, preferred_element_type=F32))
                b_out = (bg_t * dtde_t[e:e + 1, :]).astype(BF16)
                upd = upd + jnp.dot(b_out, xs_e, preferred_element_type=F32)
            ysc_ref[:, lanes] = y
            keep = jnp.where(first[0:1, :], ea_tot[:, 2 * pair:2 * pair + 1], ea_tot[:, 2 * pair + 1:2 * pair + 2])
            ht_ref[pair] = keep * ht + upd

    z_in = z_ref[...]
    for g in range(G):
        lanes = slice(g * GW, (g + 1) * GW)
        yg = ysc_ref[0:rows, lanes] * _silu(z_in[:, lanes])
        y_ref[:, lanes] = _rms(yg, gn_ref[:, lanes]).astype(y_ref.dtype)

    @pl.when(c == last)
    def _():
        for pair in range(PAIRS):
            two = ht_ref[pair].T
            hout_ref[2 * pair] = two[0:P]
            hout_ref[2 * pair + 1] = two[P:2 * P]


def _ssd(z, xbc, dt, conv_w, conv_b, dt_bias, a_log, d_skip, gate_norm, conv0=None, ssm0=None):
    b, t, _ = z.shape
    has_init = conv0 is not None
    rows = min(SSD_CHUNK, t)
    assert t % rows == 0 and (rows == SSD_CHUNK or t == rows)
    nc = t // rows
    if rows < SSD_CHUNK:
        dt = jnp.pad(dt, ((0, 0), (0, SSD_CHUNK - rows), (0, 0)))
    lane_pad = lambda v: jnp.pad(v.astype(F32), (0, LANES - v.shape[0])).reshape(1, LANES)
    tile = lambda n: pl.BlockSpec((None, rows, n), lambda bi, ci: (bi, ci, 0))
    const = lambda shape: pl.BlockSpec(shape, lambda bi, ci: (0,) * len(shape))
    state_spec = pl.BlockSpec((None, SSM_HEADS, SSM_HEADDIM, SSM_STATE), lambda bi, ci: (bi, 0, 0, 0))
    in_specs = [tile(SSM_INNER), tile(CONV_DIM), pl.BlockSpec((None, SSD_CHUNK, LANES), lambda bi, ci: (bi, ci, 0)),
                const((SSM_CONV, CONV_DIM)), const((1, CONV_DIM)),
                const((1, LANES)), const((1, LANES)), const((1, SSM_INNER)), const((1, SSM_INNER))]
    args = [z, xbc, dt, conv_w, conv_b.reshape(1, CONV_DIM), lane_pad(dt_bias), lane_pad(a_log),
            jnp.repeat(d_skip.astype(F32), SSM_HEADDIM).reshape(1, SSM_INNER), gate_norm.reshape(1, SSM_INNER)]
    if has_init:
        in_specs += [pl.BlockSpec((None, SUBLANES, CONV_DIM), lambda bi, ci: (bi, 0, 0)), state_spec]
        args += [jnp.pad(conv0, ((0, 0), (SUBLANES - (SSM_CONV - 1), 0), (0, 0))), ssm0]
    scratch = [pltpu.VMEM((SUBLANES + SSD_CHUNK, CONV_DIM), F32),
               pltpu.VMEM((SSD_CHUNK, CONV_DIM), F32),
               pltpu.VMEM((SSD_CHUNK, SSM_INNER), F32),
               pltpu.VMEM((SSM_HEADS // 2, SSM_STATE, 2 * SSM_HEADDIM), F32)]
    y_dtype = BF16 if rows % 16 == 0 else F32
    return pl.pallas_call(
        functools.partial(_ssd_kernel, rows=rows, has_init=has_init),
        out_shape=(jax.ShapeDtypeStruct((b, t, SSM_INNER), y_dtype),
                   jax.ShapeDtypeStruct((b, SSM_HEADS, SSM_HEADDIM, SSM_STATE), F32)),
        grid=(b, nc),
        in_specs=in_specs,
        out_specs=(tile(SSM_INNER), state_spec),
        scratch_shapes=scratch,
        compiler_params=_params("parallel", "arbitrary"),
        name="ssd_chunk",
    )(*args)


def kernel(x_prompt, x_sample, cache_k, cache_v, page_table, state_conv, state_ssm, attn_norm, w_qkv, q_norm, k_norm, w_o, ssm_norm, w_in, conv_w, conv_b, dt_bias, a_log, d_skip, gate_norm, w_out, ffn_norm, w_gate, w_up, w_down):
    b, s, d = x_prompt.shape
    db, t, _ = x_sample.shape
    past = page_table.shape[1] * PAGE_SIZE
    tm_p, tm_s = 512, db * t
    bf = lambda w: w.astype(BF16)

    w_qkv_bf, w_o_bf = bf(w_qkv[0]), bf(w_o[0])
    ffn = (ffn_norm, bf(w_gate), bf(w_up), bf(w_down))
    ffn0, ffn1 = ffn + (0,), ffn + (1,)
    q_p, k_p, v_p = _qkv(x_prompt, attn_norm[0], w_qkv_bf, q_norm[0], k_norm[0], jnp.arange(s), tm_p)
    attn_p = _moba_prompt(q_p, k_p, v_p)
    pos_s = past + jnp.tile(jnp.arange(t), db)
    xs_rows = x_sample.reshape(1, db * t, d)
    qkv_s = _qkv(xs_rows, attn_norm[0], w_qkv_bf, q_norm[0], k_norm[0], pos_s, tm_s)
    q_s, k_s, v_s = (jnp.transpose(a.reshape(N_HEADS, db, t, HEAD_DIM), (1, 0, 2, 3)) for a in qkv_s)
    attn_s = _moba_sample(q_s, k_s, v_s, cache_k[0], cache_v[0], page_table)
    h_p = _proj_ffn(attn_p, x_prompt.reshape(b * s, d), w_o_bf, *ffn0, tm_p)
    h_s = _proj_ffn(attn_s.reshape(db * t, d), x_sample.reshape(db * t, d), w_o_bf, *ffn0, tm_s)

    w_in1 = w_in[0]
    wz_bf = bf(w_in1[:, :SSM_INNER])
    wx_bf = bf(w_in1[:, SSM_INNER:SSM_INNER + CONV_DIM])
    wdt_bf = bf(jnp.pad(w_in1[:, SSM_INNER + CONV_DIM:], ((0, 0), (0, LANES - SSM_HEADS))))
    w_out_bf = bf(w_out[0])
    conv = (conv_w[0], conv_b[0])
    ssd_w = (dt_bias[0], a_log[0], d_skip[0], gate_norm[0])
    z_p, xbc_p, dt_p = _inproj(h_p, ssm_norm[0], wz_bf, wx_bf, wdt_bf, tm_p)
    z_s, xbc_s, dt_s = _inproj(h_s, ssm_norm[0], wz_bf, wx_bf, wdt_bf, tm_s)
    xbc_p = xbc_p.reshape(b, s, CONV_DIM)
    xbc_s = xbc_s.reshape(db, t, CONV_DIM)
    y_p, ssm_p = _ssd(z_p.reshape(b, s, SSM_INNER), xbc_p, dt_p.reshape(b, s, LANES), *conv, *ssd_w)
    y_s, ssm_s = _ssd(z_s.reshape(db, t, SSM_INNER), xbc_s, dt_s.reshape(db, t, LANES), *conv, *ssd_w,
                      conv0=state_conv[0], ssm0=state_ssm[0])
    out_p = _proj_ffn(y_p.reshape(b * s, SSM_INNER), h_p, w_out_bf, *ffn1, tm_p)
    out_s = _proj_ffn(bf(y_s).reshape(db * t, SSM_INNER), h_s, w_out_bf, *ffn1, tm_s)

    keep = SSM_CONV - 1
    conv_p = xbc_p[:, s - keep:]
    conv_s = jnp.concatenate([state_conv[0], xbc_s], axis=1)[:, t:]
    return (out_p.reshape(b, s, d), out_s.reshape(db, t, d), k_p[None], v_p[None], k_s[None], v_s[None],
            conv_p[None], ssm_p[None], conv_s[None], ssm_s[None])
```
